```python
import math
import jax, jax.numpy as jnp
from jax import lax
import numpy as np

D_MODEL = 2048
BATCH = 4
SEQ = 4096
DEPTH = 2

CTX_LEN = 256
GRID_W = 64
HEAD_DIM = 128
A_HEADS = 8
A_KV_HEADS = 2
GQA_GROUP = A_HEADS // A_KV_HEADS
B_HEADS = 4
B_QK_DIM = 64
B_V_DIM = 2 * B_QK_DIM
POOL_WINDOWS = (2, 4, 8, 16)
POOL_GROUP = 128
C_WIDTH = len(POOL_WINDOWS) * POOL_GROUP
A_Q = A_HEADS * HEAD_DIM
A_KV = A_KV_HEADS * HEAD_DIM
B_QK = B_HEADS * 2 * B_QK_DIM
B_V = B_HEADS * B_V_DIM
IN_SIZES = (A_Q, A_KV, A_KV, B_QK, B_QK, B_V, C_WIDTH)
IN_SPLITS = tuple(int(v) for v in np.cumsum(IN_SIZES)[:-1])
IN_WIDTH = sum(IN_SIZES)
MIX_WIDTH = A_Q + B_V + C_WIDTH
N_EXPERTS = 32
TOP_K = 4
D_EXPERT = 1024
SWIGLU_LIMIT = 7.0
SWIGLU_ALPHA = 1.702
ROPE_THETA = 10000.0
Q_BLOCK = 128
EPS = 1e-6

kernel_name = 'hybrid_parallel_heads_prefix_ctx_moe'


def rms_norm(x, g):
    xf = x.astype(jnp.float32)
    y = xf * lax.rsqrt(jnp.mean(xf * xf, axis=-1, keepdims=True) + EPS)
    return (y * g.astype(jnp.float32)).astype(x.dtype)


def rope_tables(pos, dim):
    inv = ROPE_THETA ** (-jnp.arange(0, dim, 2, dtype=jnp.float32) / dim)
    ang = pos.astype(jnp.float32)[:, None] * inv[None, :]
    return jnp.cos(ang), jnp.sin(ang)


def rope_1d(x, cos, sin):
    n2 = x.shape[-1] // 2
    x1, x2 = x[..., :n2], x[..., n2:]
    cs = cos[None, :, None, :].astype(x.dtype)
    sn = sin[None, :, None, :].astype(x.dtype)
    return jnp.concatenate([x1 * cs - x2 * sn, x1 * sn + x2 * cs], axis=-1)


def axial_rope(x, tabs):
    cr, sr, cc, sc = tabs
    h = x.shape[-1] // 2
    return jnp.concatenate([rope_1d(x[..., :h], cr, sr), rope_1d(x[..., h:], cc, sc)], axis=-1)


def axial_tables(row, col, head_dim):
    cr, sr = rope_tables(row, head_dim // 2)
    cc, sc = rope_tables(col, head_dim // 2)
    return (cr, sr, cc, sc)


def sweep_query_blocks(q, fn):
    b, s = q.shape[:2]
    nb = s // Q_BLOCK
    qb = jnp.moveaxis(q.reshape((b, nb, Q_BLOCK) + q.shape[2:]), 1, 0)
    out = jnp.moveaxis(lax.map(fn, qb), 0, 1)
    return out.reshape((b, s) + out.shape[3:])


def gqa_attend(q, k, v):
    s = jnp.einsum('bqkgd,btkd->bkgqt', q, k).astype(jnp.float32) * (1.0 / math.sqrt(HEAD_DIM))
    p = jax.nn.softmax(s, axis=-1).astype(v.dtype)
    return jnp.einsum('bkgqt,btkd->bqkgd', p, v)


def diff_attend(q, k, v, lam):
    s = jnp.einsum('bqchd,btchd->bchqt', q, k).astype(jnp.float32) * (1.0 / math.sqrt(B_QK_DIM))
    p = jax.nn.softmax(s, axis=-1)
    a = (p[:, 0] - lam * p[:, 1]).astype(v.dtype)
    return jnp.einsum('bhqt,bthe->bqhe', a, v)


def multiscale_pool(u, w_pool, pool_scale):
    b, L, cw = u.shape
    uf = u.astype(jnp.float32)
    cs = jnp.concatenate([jnp.zeros((b, 1, cw), jnp.float32), jnp.cumsum(uf, axis=1)], axis=1)
    t = jnp.arange(L)
    groups = []
    for gi, w in enumerate(POOL_WINDOWS):
        lo = jnp.clip(t - w // 2, 0, L)
        hi = jnp.clip(t + w // 2, 0, L)
        seg = cs[:, :, gi * POOL_GROUP:(gi + 1) * POOL_GROUP]
        mean = (seg[:, hi] - seg[:, lo]) / (hi - lo).astype(jnp.float32)[None, :, None]
        groups.append(mean - uf[..., gi * POOL_GROUP:(gi + 1) * POOL_GROUP])
    p = jnp.stack(groups, axis=2).astype(u.dtype)
    y = jnp.einsum('blgc,gcd->blgd', p, w_pool).reshape(b, L, cw)
    return y * pool_scale


def project(h, w_in, g_q, g_k):
    b, L, _ = h.shape
    p = jnp.einsum('bld,de->ble', h, w_in)
    qa, ka, va, qb, kb, vb, uc = jnp.split(p, IN_SPLITS, axis=-1)
    qa = rms_norm(qa.reshape(b, L, A_HEADS, HEAD_DIM), g_q)
    ka = rms_norm(ka.reshape(b, L, A_KV_HEADS, HEAD_DIM), g_k)
    va = va.reshape(b, L, A_KV_HEADS, HEAD_DIM)
    qb = qb.reshape(b, L, 2 * B_HEADS, B_QK_DIM)
    kb = kb.reshape(b, L, 2 * B_HEADS, B_QK_DIM)
    vb = vb.reshape(b, L, B_HEADS, B_V_DIM)
    return qa, ka, va, qb, kb, vb, uc


def mix_layer(h_ctx, h_lat, rope_a, rope_b, w_in, g_q, g_k, lam, lam_init, g_sub, w_pool, pool_scale, w_out, need_ctx):
    b, S, _ = h_lat.shape
    C = h_ctx.shape[1]
    T = C + S
    qa_c, ka_c, va_c, qb_c, kb_c, vb_c, uc_c = project(h_ctx, w_in, g_q, g_k)
    qa_l, ka_l, va_l, qb_l, kb_l, vb_l, uc_l = project(h_lat, w_in, g_q, g_k)
    qa_l, ka_l = axial_rope(qa_l, rope_a), axial_rope(ka_l, rope_a)
    qb_l, kb_l = axial_rope(qb_l, rope_b), axial_rope(kb_l, rope_b)
    ka_all = jnp.concatenate([ka_c, ka_l], axis=1)
    va_all = jnp.concatenate([va_c, va_l], axis=1)
    qa5 = qa_l.reshape(b, S, A_KV_HEADS, GQA_GROUP, HEAD_DIM)
    oa = sweep_query_blocks(qa5, lambda qblk: gqa_attend(qblk, ka_all, va_all)).reshape(b, S, A_Q)
    kb_all = jnp.concatenate([kb_c, kb_l], axis=1).reshape(b, T, 2, B_HEADS, B_QK_DIM)
    vb_all = jnp.concatenate([vb_c, vb_l], axis=1)
    qb5 = qb_l.reshape(b, S, 2, B_HEADS, B_QK_DIM)
    ob = sweep_query_blocks(qb5, lambda qblk: diff_attend(qblk, kb_all, vb_all, lam))
    ob = (rms_norm(ob, g_sub) * (1.0 - lam_init)).reshape(b, S, B_V)
    oc = multiscale_pool(uc_l, w_pool, pool_scale)
    out_lat = jnp.einsum('ble,ed->bld', jnp.concatenate([oa, ob, oc], axis=-1), w_out)
    if not need_ctx:
        return None, out_lat
    oa_c = gqa_attend(qa_c.reshape(b, C, A_KV_HEADS, GQA_GROUP, HEAD_DIM), ka_c, va_c).reshape(b, C, A_Q)
    ob_c = diff_attend(qb_c.reshape(b, C, 2, B_HEADS, B_QK_DIM), kb_c.reshape(b, C, 2, B_HEADS, B_QK_DIM), vb_c, lam)
    ob_c = (rms_norm(ob_c, g_sub) * (1.0 - lam_init)).reshape(b, C, B_V)
    oc_c = multiscale_pool(uc_c, w_pool, pool_scale)
    out_ctx = jnp.einsum('ble,ed->bld', jnp.concatenate([oa_c, ob_c, oc_c], axis=-1), w_out)
    return out_ctx, out_lat


def moe(h, w_router, b_router, w1, b1, w2, b2):
    logits = jnp.einsum('nd,de->ne', h, w_router).astype(jnp.float32) + b_router.astype(jnp.float32)
    top_v, top_i = lax.top_k(logits, TOP_K)
    wts = jax.nn.softmax(top_v, axis=-1)
    gates = jnp.einsum('nk,nke->ne', wts, jax.nn.one_hot(top_i, N_EXPERTS, dtype=jnp.float32))
    out = jnp.zeros(h.shape, jnp.float32)
    for e in range(N_EXPERTS):
        u = jnp.einsum('nd,df->nf', h, w1[e]) + b1[e]
        glu = jnp.minimum(u[..., ::2], SWIGLU_LIMIT)
        lin = jnp.clip(u[..., 1::2], -SWIGLU_LIMIT, SWIGLU_LIMIT)
        a = glu * jax.nn.sigmoid(SWIGLU_ALPHA * glu) * (lin + 1.0)
        y = jnp.einsum('nf,fd->nd', a, w2[e]) + b2[e]
        out = out + gates[:, e:e + 1] * y.astype(jnp.float32)
    return out.astype(h.dtype)


def setup_inputs(seed: int = 0) -> dict:
    key = jax.random.key(seed)
    ks = iter(jax.random.split(key, 32))
    f32 = jnp.float32

    def nrm(shape, scale):
        return jax.random.normal(next(ks), shape, f32) * scale

    L, D, E, F = DEPTH, D_MODEL, N_EXPERTS, D_EXPERT
    return {
        'x': nrm((BATCH, SEQ, D), 1.0),
        'c': nrm((BATCH, D), 1.0),
        'ctx': nrm((BATCH, CTX_LEN, D), 1.0),
        'c_ctx': nrm((D,), 1.0),
        'w_mod': nrm((L, D, 6 * D), 0.5 * D ** -0.5),
        'b_mod': nrm((L, 6 * D), 0.01),
        'g_norm1': 1.0 + nrm((L, D), 0.05),
        'g_norm2': 1.0 + nrm((L, D), 0.05),
        'w_in': nrm((L, D, IN_WIDTH), D ** -0.5),
        'g_qnorm': 1.0 + nrm((L, HEAD_DIM), 0.05),
        'g_knorm': 1.0 + nrm((L, HEAD_DIM), 0.05),
        'lambda_q1': nrm((L, B_QK_DIM), 0.1),
        'lambda_k1': nrm((L, B_QK_DIM), 0.1),
        'lambda_q2': nrm((L, B_QK_DIM), 0.1),
        'lambda_k2': nrm((L, B_QK_DIM), 0.1),
        'g_subln': 1.0 + nrm((L, B_V_DIM), 0.05),
        'w_pool': nrm((L, len(POOL_WINDOWS), POOL_GROUP, POOL_GROUP), POOL_GROUP ** -0.5),
        'pool_scale': 1.0 + nrm((L, C_WIDTH), 0.05),
        'w_out': nrm((L, MIX_WIDTH, D), MIX_WIDTH ** -0.5),
        'w_router': nrm((L, D, E), D ** -0.5),
        'b_router': nrm((L, E), 0.01),
        'w_expert_in': nrm((L, E, D, 2 * F), D ** -0.5),
        'b_expert_in': nrm((L, E, 2 * F), 0.01),
        'w_expert_out': nrm((L, E, F, D), F ** -0.5),
        'b_expert_out': nrm((L, E, D), 0.01),
        'g_final': 1.0 + nrm((D,), 0.05),
    }


def reference(x, c, ctx, c_ctx, w_mod, b_mod, g_norm1, g_norm2, w_in, g_qnorm, g_knorm, lambda_q1, lambda_k1, lambda_q2, lambda_k2, g_subln, w_pool, pool_scale, w_out, w_router, b_router, w_expert_in, b_expert_in, w_expert_out, b_expert_out, g_final):
    b, S, D = x.shape
    C = ctx.shape[1]
    rows = S // GRID_W
    row = jnp.repeat(jnp.arange(rows), GRID_W)
    col = jnp.tile(jnp.arange(GRID_W), rows)
    rope_a = axial_tables(row, col, HEAD_DIM)
    rope_b = axial_tables(row, col, B_QK_DIM)
    for l in range(DEPTH):
        need_ctx = l < DEPTH - 1
        lam_init = 0.8 - 0.6 * math.exp(-0.3 * l)
        lam = (jnp.exp(jnp.sum(lambda_q1[l].astype(jnp.float32) * lambda_k1[l].astype(jnp.float32)))
               - jnp.exp(jnp.sum(lambda_q2[l].astype(jnp.float32) * lambda_k2[l].astype(jnp.float32)))
               + lam_init)
        mod_l = (jnp.einsum('bd,de->be', jax.nn.silu(c), w_mod[l]) + b_mod[l])[:, None, :]
        mod_c = (jnp.einsum('d,de->e', jax.nn.silu(c_ctx), w_mod[l]) + b_mod[l])[None, None, :]
        sh1, sc1, gt1, sh2, sc2, gt2 = jnp.split(mod_l, 6, axis=-1)
        csh1, csc1, cgt1, csh2, csc2, cgt2 = jnp.split(mod_c, 6, axis=-1)
        n_lat = rms_norm(x, g_norm1[l]) * (1.0 + sc1) + sh1
        n_ctx = rms_norm(ctx, g_norm1[l]) * (1.0 + csc1) + csh1
        m_ctx, m_lat = mix_layer(n_ctx, n_lat, rope_a, rope_b, w_in[l], g_qnorm[l], g_knorm[l], lam, lam_init,
                                 g_subln[l], w_pool[l], pool_scale[l], w_out[l], need_ctx)
        x = x + gt1 * m_lat
        f_lat = rms_norm(x, g_norm2[l]) * (1.0 + sc2) + sh2
        if need_ctx:
            ctx = ctx + cgt1 * m_ctx
            f_ctx = rms_norm(ctx, g_norm2[l]) * (1.0 + csc2) + csh2
            tokens = jnp.concatenate([f_lat.reshape(-1, D), f_ctx.reshape(-1, D)], axis=0)
        else:
            tokens = f_lat.reshape(-1, D)
        y = moe(tokens, w_router[l], b_router[l], w_expert_in[l], b_expert_in[l], w_expert_out[l], b_expert_out[l])
        x = x + gt2 * y[:b * S].reshape(b, S, D)
        if need_ctx:
            ctx = ctx + cgt2 * y[b * S:].reshape(b, C, D)
    return rms_norm(x, g_final)
```

```python
import functools
import math

import numpy as np
import jax
import jax.numpy as jnp
from jax import lax
from jax.experimental import pallas as pl
from jax.experimental.pallas import tpu as pltpu

F32 = jnp.float32
BF16 = jnp.bfloat16
U32 = jnp.uint32
I32 = jnp.int32

HEAD_DIM = 128
A_HEADS = 8
A_KV_HEADS = 2
GQA_GROUP = A_HEADS // A_KV_HEADS
B_HEADS = 4
B_QK_DIM = 64
POOL_WINDOWS = (2, 4, 8, 16)
POOL_GROUP = 128
C_WIDTH = len(POOL_WINDOWS) * POOL_GROUP
A_Q = A_HEADS * HEAD_DIM
A_KV = A_KV_HEADS * HEAD_DIM
B_QK = B_HEADS * 2 * B_QK_DIM
B_V = B_HEADS * 2 * B_QK_DIM
N_EXPERTS = 32
TOP_K = 4
SWIGLU_LIMIT = 7.0
SWIGLU_ALPHA = 1.702
ROPE_THETA = 10000.0
GRID_W = 64
EPS = 1e-6

TM = 256
POOL_HALO = 8
LANES = 128
VMEM_LIMIT = 56 * 1024 * 1024
LOG2E = math.log2(math.e)
NEG_BIG = -1e30


def _cparams(sem, vmem=None):
    return pltpu.CompilerParams(dimension_semantics=sem, vmem_limit_bytes=vmem)


def _sigmoid(z):
    return 1.0 / (1.0 + jnp.exp(-z))


def _pack_bf16_pair(lo, hi):
    lo_b = lax.bitcast_convert_type(lo.astype(BF16).astype(F32), U32)
    hi_b = lax.bitcast_convert_type(hi.astype(BF16).astype(F32), U32)
    return (hi_b & jnp.uint32(0xFFFF0000)) | (lo_b >> 16)


def _unpack_bf16_pair(p):
    lo = lax.bitcast_convert_type(p << 16, F32)
    hi = lax.bitcast_convert_type(p & jnp.uint32(0xFFFF0000), F32)
    return lo, hi


def _mod_kernel(cc_ref, w_ref, b_ref, o_ref):
    cc = cc_ref[...]
    a = (cc * _sigmoid(cc)).astype(BF16)
    o_ref[0] = jnp.dot(a, w_ref[0].astype(BF16), preferred_element_type=F32) + b_ref[0]


def _modulation(cc, w_mod, b_mod):
    depth, d, n = w_mod.shape
    tn = 1536
    return pl.pallas_call(
        _mod_kernel,
        grid=(depth, n // tn),
        in_specs=[
            pl.BlockSpec((8, d), lambda l, j: (0, 0)),
            pl.BlockSpec((1, d, tn), lambda l, j: (l, 0, j)),
            pl.BlockSpec((1, 1, tn), lambda l, j: (l, 0, j)),
        ],
        out_specs=pl.BlockSpec((1, 8, tn), lambda l, j: (l, 0, j)),
        out_shape=jax.ShapeDtypeStruct((depth, 8, n), F32),
        compiler_params=_cparams(("parallel", "parallel"), VMEM_LIMIT),
        name="modulation",
    )(cc, w_mod, b_mod.reshape(depth, 1, n))


def _rope(y, cos, sin_signed, half):
    width = y.shape[-1]
    lane = lax.broadcasted_iota(I32, y.shape, 1)
    partner = jnp.where(lane % (2 * half) < half,
                        pltpu.roll(y, width - half, 1), pltpu.roll(y, half, 1))
    return y * cos + partner * sin_signed


def _head_rms(y, g):
    return y * lax.rsqrt(jnp.mean(y * y, axis=-1, keepdims=True) + EPS) * g


def _proj_kernel(x_ref, mod_ref, g1_ref, w_ref, gq_ref, gk_ref, ca_ref, sa_ref, cb_ref, sb_ref,
                 qa_ref, ka_ref, vat_ref, qb_ref, kb_ref, vbt_ref, uc_ref):
    d = x_ref.shape[-1]
    x = x_ref[0]
    xn = x * lax.rsqrt(jnp.mean(x * x, axis=-1, keepdims=True) + EPS) * g1_ref[...]
    sh = mod_ref[0, :, 0:d]
    sc = mod_ref[0, :, d:2 * d]
    h = (xn * (1.0 + sc) + sh).astype(BF16)
    ca, sa, cb, sb = ca_ref[...], sa_ref[...], cb_ref[...], sb_ref[...]

    c0 = 0
    pq = jnp.dot(h, w_ref[:, c0:c0 + A_Q], preferred_element_type=F32)
    for hd in range(A_HEADS):
        y = _head_rms(pq[:, hd * HEAD_DIM:(hd + 1) * HEAD_DIM], gq_ref[...])
        qa_ref[0, :, hd * HEAD_DIM:(hd + 1) * HEAD_DIM] = _rope(y, ca, sa, HEAD_DIM // 4).astype(BF16)
    c0 += A_Q
    pkv = jnp.dot(h, w_ref[:, c0:c0 + 2 * A_KV], preferred_element_type=F32)
    for hd in range(A_KV_HEADS):
        y = _head_rms(pkv[:, hd * HEAD_DIM:(hd + 1) * HEAD_DIM], gk_ref[...])
        ka_ref[0, :, hd * HEAD_DIM:(hd + 1) * HEAD_DIM] = _rope(y, ca, sa, HEAD_DIM // 4).astype(BF16)
    vat_ref[0] = pkv[:, A_KV:2 * A_KV].T.astype(BF16)
    c0 += 2 * A_KV
    pb = jnp.dot(h, w_ref[:, c0:c0 + 2 * B_QK], preferred_element_type=F32)
    for hd in range(B_HEADS):
        sl = slice(hd * LANES, (hd + 1) * LANES)
        qb_ref[0, :, sl] = _rope(pb[:, sl], cb, sb, B_QK_DIM // 4).astype(BF16)
        kb_ref[0, :, sl] = _rope(pb[:, B_QK + hd * LANES:B_QK + (hd + 1) * LANES], cb, sb,
                                 B_QK_DIM // 4).astype(BF16)
    c0 += 2 * B_QK
    pvu = jnp.dot(h, w_ref[:, c0:c0 + B_V + C_WIDTH], preferred_element_type=F32)
    vbt_ref[0] = pvu[:, 0:B_V].T.astype(BF16)
    uc_ref[0] = pvu[:, B_V:B_V + C_WIDTH]


def _in_projection(xa, mod3, g1, w_in_p, gq, gk, rope):
    b, t, d = xa.shape
    nt = t // TM
    n_in = w_in_p.shape[1]
    tok = lambda w: pl.BlockSpec((1, TM, w), lambda bb, i: (bb, i, 0))
    tok_t = lambda w: pl.BlockSpec((1, w, TM), lambda bb, i: (bb, 0, i))
    const2 = lambda r, c: pl.BlockSpec((r, c), lambda bb, i: (0, 0))
    tab = pl.BlockSpec((TM, LANES), lambda bb, i: (i, 0))
    return pl.pallas_call(
        _proj_kernel,
        grid=(b, nt),
        in_specs=[
            tok(d),
            pl.BlockSpec((1, 1, mod3.shape[-1]), lambda bb, i: (jnp.where(i == 0, b, bb), 0, 0)),
            const2(1, d), const2(d, n_in), const2(1, HEAD_DIM), const2(1, HEAD_DIM),
            tab, tab, tab, tab,
        ],
        out_specs=[tok(A_Q), tok(A_KV), tok_t(A_KV), tok(B_QK), tok(B_QK), tok_t(B_V), tok(C_WIDTH)],
        out_shape=[
            jax.ShapeDtypeStruct((b, t, A_Q), BF16),
            jax.ShapeDtypeStruct((b, t, A_KV), BF16),
            jax.ShapeDtypeStruct((b, A_KV, t), BF16),
            jax.ShapeDtypeStruct((b, t, B_QK), BF16),
            jax.ShapeDtypeStruct((b, t, B_QK), BF16),
            jax.ShapeDtypeStruct((b, B_V, t), BF16),
            jax.ShapeDtypeStruct((b, t, C_WIDTH), F32),
        ],
        compiler_params=_cparams(("parallel", "parallel"), VMEM_LIMIT),
        name="in_projection",
    )(xa, mod3, g1, w_in_p, gq, gk, *rope)


def _flash_t(q_stack, k_ref, vt_ref, n_kv, scale):
    mq = q_stack.shape[0]
    c = scale * LOG2E

    def body(j, carry):
        m, l, acc = carry
        off = pl.multiple_of(j * TM, TM)
        k = k_ref[0, pl.ds(off, TM), :]
        vt = vt_ref[0, :, pl.ds(off, TM)]
        s = lax.dot_general(k, q_stack, (((1,), (1,)), ((), ())), preferred_element_type=F32)
        m_new = jnp.maximum(m, jnp.max(s, axis=0, keepdims=True))
        alpha = jnp.exp2((m - m_new) * c)
        p = jnp.exp2((s - m_new) * c)
        l = alpha * l + jnp.sum(p, axis=0, keepdims=True)
        acc = alpha * acc + jnp.dot(vt, p.astype(BF16), preferred_element_type=F32)
        return m_new, l, acc

    init = (jnp.full((1, mq), NEG_BIG, F32), jnp.zeros((1, mq), F32), jnp.zeros((HEAD_DIM, mq), F32))
    _, l, acc = lax.fori_loop(0, n_kv, body, init)
    return acc, l


def _attn_a_kernel(q_ref, k_ref, vt_ref, o_ref, *, q0, n_kv_full):
    qi = pl.program_id(2) + q0
    n_kv = jnp.where(qi == 0, 1, n_kv_full)
    q = q_ref[0]
    q_stack = jnp.concatenate([q[:, g * HEAD_DIM:(g + 1) * HEAD_DIM] for g in range(GQA_GROUP)], axis=0)
    acc, l = _flash_t(q_stack, k_ref, vt_ref, n_kv, 1.0 / math.sqrt(HEAD_DIM))
    o = acc / l
    for g in range(GQA_GROUP):
        o_ref[0, :, g * HEAD_DIM:(g + 1) * HEAD_DIM] = o[:, g * TM:(g + 1) * TM].T.astype(BF16)


def _attn_b_kernel(q_ref, k_ref, vt_ref, lam_ref, gs_ref, o_ref, *, q0, n_kv_full, lam_init):
    qi = pl.program_id(2) + q0
    n_kv = jnp.where(qi == 0, 1, n_kv_full)
    q = q_ref[0]
    lane = lax.broadcasted_iota(I32, q.shape, 1)
    zero = jnp.zeros_like(q)
    q_stack = jnp.concatenate([jnp.where(lane < B_QK_DIM, q, zero), jnp.where(lane >= B_QK_DIM, q, zero)], axis=0)
    acc, l = _flash_t(q_stack, k_ref, vt_ref, n_kv, 1.0 / math.sqrt(B_QK_DIM))
    o = acc / l
    lv = lam_ref[...]
    lam = (jnp.exp(jnp.sum(lv[0:1] * lv[1:2], axis=-1, keepdims=True))
           - jnp.exp(jnp.sum(lv[2:3] * lv[3:4], axis=-1, keepdims=True)) + lam_init)
    od = o[:, 0:TM] - lam * o[:, TM:2 * TM]
    r = lax.rsqrt(jnp.mean(od * od, axis=0, keepdims=True) + EPS)
    y = od * r * gs_ref[...] * (1.0 - lam_init)
    o_ref[0] = y.T.astype(BF16)


def _attention_a(qa, ka, vat, q0):
    b, t, _ = qa.shape
    nt = t // TM
    gw = GQA_GROUP * HEAD_DIM
    return pl.pallas_call(
        functools.partial(_attn_a_kernel, q0=q0, n_kv_full=nt),
        grid=(b, A_KV_HEADS, nt - q0),
        in_specs=[
            pl.BlockSpec((1, TM, gw), lambda bb, h, i: (bb, i + q0, h)),
            pl.BlockSpec((1, t, HEAD_DIM), lambda bb, h, i: (bb, 0, h)),
            pl.BlockSpec((1, HEAD_DIM, t), lambda bb, h, i: (bb, h, 0)),
        ],
        out_specs=pl.BlockSpec((1, TM, gw), lambda bb, h, i: (bb, i, h)),
        out_shape=jax.ShapeDtypeStruct((b, t - q0 * TM, A_Q), BF16),
        compiler_params=_cparams(("parallel", "parallel", "parallel"), VMEM_LIMIT),
        name="attention_gqa",
    )(qa, ka, vat)


def _attention_b(qb, kb, vbt, lam_vecs, g_sub_col, q0, lam_init):
    b, t, _ = qb.shape
    nt = t // TM
    return pl.pallas_call(
        functools.partial(_attn_b_kernel, q0=q0, n_kv_full=nt, lam_init=lam_init),
        grid=(b, B_HEADS, nt - q0),
        in_specs=[
            pl.BlockSpec((1, TM, LANES), lambda bb, h, i: (bb, i + q0, h)),
            pl.BlockSpec((1, t, LANES), lambda bb, h, i: (bb, 0, h)),
            pl.BlockSpec((1, LANES, t), lambda bb, h, i: (bb, h, 0)),
            pl.BlockSpec((4, B_QK_DIM), lambda bb, h, i: (0, 0)),
            pl.BlockSpec((LANES, 1), lambda bb, h, i: (0, 0)),
        ],
        out_specs=pl.BlockSpec((1, TM, LANES), lambda bb, h, i: (bb, i, h)),
        out_shape=jax.ShapeDtypeStruct((b, t - q0 * TM, B_V), BF16),
        compiler_params=_cparams(("parallel", "parallel", "parallel"), VMEM_LIMIT),
        name="attention_diff",
    )(qb, kb, vbt, lam_vecs, g_sub_col)


def _pool_kernel(prev_ref, cur_ref, next_ref, wp_ref, ps_ref, o_ref, *, q0, ctx_len, total_len):
    i = pl.program_id(1) + q0
    ext = jnp.concatenate([prev_ref[0], cur_ref[0], next_ref[0]], axis=0)
    rows = ext.shape[0]
    seg_lo = jnp.where(i == 0, 0, ctx_len)
    seg_hi = jnp.where(i == 0, ctx_len, total_len)
    grow = i * TM - POOL_HALO + lax.broadcasted_iota(I32, (rows, 1), 0)
    ext = jnp.where((grow >= seg_lo) & (grow < seg_hi), ext, 0.0)
    tpos = grow[POOL_HALO:POOL_HALO + TM] - seg_lo
    seg_len = seg_hi - seg_lo

    def back(a, d):
        return pltpu.roll(a, d, 0)

    def fwd(a, d):
        return pltpu.roll(a, rows - d, 0)

    for gi, w in enumerate(POOL_WINDOWS):
        u = ext[:, gi * POOL_GROUP:(gi + 1) * POOL_GROUP]
        acc = u + back(u, 1)
        span = 2
        while span < w:
            acc = acc + back(acc, span)
            span *= 2
        if w > 2:
            acc = fwd(acc, w // 2 - 1)
        cnt = (jnp.minimum(tpos + w // 2, seg_len) - jnp.maximum(tpos - w // 2, 0)).astype(F32)
        sl = slice(POOL_HALO, POOL_HALO + TM)
        p = (acc[sl] / cnt - u[sl]).astype(BF16)
        y = jnp.dot(p, wp_ref[gi], preferred_element_type=F32)
        o_ref[0, :, gi * POOL_GROUP:(gi + 1) * POOL_GROUP] = (
            y * ps_ref[:, gi * POOL_GROUP:(gi + 1) * POOL_GROUP]).astype(BF16)


def _pooling(uc, w_pool_b, pool_scale, q0, ctx_len):
    b, t, cw = uc.shape
    nt = t // TM
    per = TM // POOL_HALO
    last = t // POOL_HALO - 1
    return pl.pallas_call(
        functools.partial(_pool_kernel, q0=q0, ctx_len=ctx_len, total_len=t),
        grid=(b, nt - q0),
        in_specs=[
            pl.BlockSpec((1, POOL_HALO, cw), lambda bb, i: (bb, jnp.maximum((i + q0) * per - 1, 0), 0)),
            pl.BlockSpec((1, TM, cw), lambda bb, i: (bb, i + q0, 0)),
            pl.BlockSpec((1, POOL_HALO, cw), lambda bb, i: (bb, jnp.minimum((i + q0 + 1) * per, last), 0)),
            pl.BlockSpec(w_pool_b.shape, lambda bb, i: (0, 0, 0)),
            pl.BlockSpec((1, cw), lambda bb, i: (0, 0)),
        ],
        out_specs=pl.BlockSpec((1, TM, cw), lambda bb, i: (bb, i, 0)),
        out_shape=jax.ShapeDtypeStruct((b, t - q0 * TM, cw), BF16),
        compiler_params=_cparams(("parallel", "parallel")),
        name="pooling",
    )(uc, uc, uc, w_pool_b, pool_scale)


def _post_kernel(oa_ref, ob_ref, oc_ref, x_ref, mod_ref, g2_ref, wo_ref, wr_ref, br_ref,
                 xmid_ref, fpk_ref, te_ref, rk_ref, gt_ref, cnt_ref, carry_ref):
    d = x_ref.shape[-1]
    first = (pl.program_id(0) == 0) & (pl.program_id(1) == 0)

    @pl.when(first)
    def _():
        carry_ref[...] = jnp.zeros_like(carry_ref)

    m = (jnp.dot(oa_ref[0], wo_ref[0:A_Q], preferred_element_type=F32)
         + jnp.dot(ob_ref[0], wo_ref[A_Q:A_Q + B_V], preferred_element_type=F32)
         + jnp.dot(oc_ref[0], wo_ref[A_Q + B_V:A_Q + B_V + C_WIDTH], preferred_element_type=F32))
    x = x_ref[0] + mod_ref[0, :, 2 * d:3 * d] * m
    xmid_ref[0] = x
    xn = x * lax.rsqrt(jnp.mean(x * x, axis=-1, keepdims=True) + EPS) * g2_ref[...]
    f = xn * (1.0 + mod_ref[0, :, 4 * d:5 * d]) + mod_ref[0, :, 3 * d:4 * d]
    half = d // 2
    fpk_ref[0] = _pack_bf16_pair(f[:, 0:half], f[:, half:d])

    f_hi = f.astype(BF16)
    f_lo = (f - f_hi.astype(F32)).astype(BF16)
    lg2 = jnp.dot(f_hi, wr_ref[...], preferred_element_type=F32)
    lg1 = jnp.dot(f_lo, wr_ref[:, 0:LANES], preferred_element_type=F32)
    logits = lg2[:, 0:LANES] + lg2[:, LANES:2 * LANES] + lg1 + br_ref[...]
    work = logits.T[0:N_EXPERTS]

    e_id = lax.broadcasted_iota(I32, work.shape, 0).astype(F32)
    vals, idxs, hots = [], [], []
    for _ in range(TOP_K):
        mx = jnp.max(work, axis=0, keepdims=True)
        idx = jnp.min(jnp.where(work == mx, e_id, float(N_EXPERTS)), axis=0, keepdims=True)
        hot = e_id == idx
        vals.append(mx)
        idxs.append(idx)
        hots.append(hot)
        work = jnp.where(hot, -jnp.inf, work)
    ex = [jnp.exp(v - vals[0]) for v in vals]
    den = ex[0] + ex[1] + ex[2] + ex[3]
    gates = [e / den for e in ex]

    msel = jnp.zeros(work.shape, F32)
    for hot in hots:
        msel = msel + jnp.where(hot, 1.0, 0.0)
    r_i = lax.broadcasted_iota(I32, (TM, TM), 0)
    c_i = lax.broadcasted_iota(I32, (TM, TM), 1)
    tri = jnp.where(r_i < c_i, 1.0, 0.0).astype(BF16)
    carry = carry_ref[:, 0:1]
    rank_full = jnp.dot(msel.astype(BF16), tri, preferred_element_type=F32) + carry
    ranks = [jnp.sum(jnp.where(hot, rank_full, 0.0), axis=0, keepdims=True) for hot in hots]
    new_carry = carry + jnp.sum(msel, axis=1, keepdims=True)
    carry_ref[...] = jnp.broadcast_to(new_carry, carry_ref.shape)
    cnt_ref[...] = jnp.broadcast_to(new_carry, cnt_ref.shape)

    row8 = lax.broadcasted_iota(I32, (8, TM), 0)

    def rows8(vs):
        out = jnp.zeros((8, TM), F32)
        for k, v in enumerate(vs):
            out = jnp.where(row8 == k, v, out)
        return out

    te_ref[...] = rows8(idxs).astype(I32)
    rk_ref[...] = rows8(ranks).astype(I32)
    g128 = jnp.concatenate([rows8(gates), jnp.zeros((LANES - 8, TM), F32)], axis=0)
    gt_ref[...] = g128.T


def _post_attention(oa, ob, oc, xa, mod3, g2, w_out_b, wr2, br, q0):
    b, t, d = xa.shape
    nt = t // TM - q0
    ntok = b * nt * TM
    tok = lambda w: pl.BlockSpec((1, TM, w), lambda bb, i: (bb, i, 0))
    const2 = lambda r, c: pl.BlockSpec((r, c), lambda bb, i: (0, 0))
    flat = lambda r: pl.BlockSpec((r, TM), lambda bb, i: (0, bb * nt + i))
    return pl.pallas_call(
        _post_kernel,
        grid=(b, nt),
        in_specs=[
            tok(A_Q), tok(B_V), tok(C_WIDTH),
            pl.BlockSpec((1, TM, d), lambda bb, i: (bb, i + q0, 0)),
            pl.BlockSpec((1, 1, mod3.shape[-1]), lambda bb, i: (jnp.where(i + q0 == 0, b, bb), 0, 0)),
            const2(1, d), const2(d, d), const2(d, 2 * LANES), const2(1, LANES),
        ],
        out_specs=[
            tok(d), tok(d // 2), flat(8), flat(8),
            pl.BlockSpec((TM, LANES), lambda bb, i: (bb * nt + i, 0)),
            const2(N_EXPERTS, LANES),
        ],
        out_shape=[
            jax.ShapeDtypeStruct((b, nt * TM, d), F32),
            jax.ShapeDtypeStruct((b, nt * TM, d // 2), U32),
            jax.ShapeDtypeStruct((8, ntok), I32),
            jax.ShapeDtypeStruct((8, ntok), I32),
            jax.ShapeDtypeStruct((ntok, LANES), F32),
            jax.ShapeDtypeStruct((N_EXPERTS, LANES), F32),
        ],
        scratch_shapes=[pltpu.VMEM((N_EXPERTS, LANES), F32)],
        compiler_params=_cparams(("arbitrary", "arbitrary"), VMEM_LIMIT),
        name="out_projection_router",
    )(oa, ob, oc, xa, mod3, g2, w_out_b, wr2, br)


def _row_copies(pos_ref, src_row, dst_row, sem):
    def issue(t, carry):
        for k in range(TOP_K):
            p = pos_ref[k * TM + t]
            pltpu.make_async_copy(src_row(k, t, p), dst_row(k, t, p), sem).start()
        return carry

    lax.fori_loop(0, TM, issue, 0, unroll=8)


def _dispatch_kernel(pos_ref, f_ref, xs_in_ref, xs_ref, sem):
    del xs_in_ref
    _row_copies(pos_ref,
                lambda k, t, p: f_ref.at[0, pl.ds(t, 1)],
                lambda k, t, p: xs_ref.at[pl.ds(p, 1)], sem)
    for k in range(TOP_K):
        pltpu.make_async_copy(f_ref.at[0], xs_ref.at[pl.ds(0, TM)], sem).wait()


def _dispatch(pos_flat, fpk, n_rows):
    b, t, hw = fpk.shape
    nt = t // TM
    return pl.pallas_call(
        _dispatch_kernel,
        grid=(b, nt),
        in_specs=[
            pl.BlockSpec((TOP_K * TM,), lambda bb, i: (bb * nt + i,), memory_space=pltpu.SMEM),
            pl.BlockSpec((1, TM, hw), lambda bb, i: (bb, i, 0)),
            pl.BlockSpec(memory_space=pl.ANY),
        ],
        out_specs=pl.BlockSpec(memory_space=pl.ANY),
        out_shape=jax.ShapeDtypeStruct((n_rows, hw), U32),
        scratch_shapes=[pltpu.SemaphoreType.DMA(())],
        input_output_aliases={2: 0},
        compiler_params=_cparams(("arbitrary", "arbitrary")),
        name="moe_dispatch",
    )(pos_flat, fpk, jnp.zeros((n_rows, hw), U32))


def _experts_kernel(te_ref, tb_ref, nu_ref, x_ref, w1_ref, b1_ref, w2_ref, b2_ref, y_ref):
    del te_ref, tb_ref
    fdim = w2_ref.shape[1]

    @pl.when(pl.program_id(0) >= nu_ref[0])
    def _():
        y_ref[...] = jnp.zeros_like(y_ref)

    @pl.when(pl.program_id(0) < nu_ref[0])
    def _():
        lo, hi = _unpack_bf16_pair(x_ref[...])
        half = lo.shape[1]
        u = (jnp.dot(lo.astype(BF16), w1_ref[0, 0:half, :], preferred_element_type=F32)
             + jnp.dot(hi.astype(BF16), w1_ref[0, half:2 * half, :], preferred_element_type=F32)
             + b1_ref[0])
        glu = jnp.minimum(u[:, 0:fdim], SWIGLU_LIMIT)
        lin = jnp.clip(u[:, fdim:2 * fdim], -SWIGLU_LIMIT, SWIGLU_LIMIT)
        a = glu * _sigmoid(SWIGLU_ALPHA * glu) * (lin + 1.0)
        y = jnp.dot(a.astype(BF16), w2_ref[0], preferred_element_type=F32) + b2_ref[0]
        y_ref[...] = _pack_bf16_pair(y[:, 0:half], y[:, half:2 * half])


def _experts(tile_expert, tile_block, n_used, xs, w1p, b1p, w2b, b2):
    n_rows, hw = xs.shape
    n_tiles = n_rows // TM
    e, d, f2 = w1p.shape
    fdim = f2 // 2
    grid_spec = pltpu.PrefetchScalarGridSpec(
        num_scalar_prefetch=3,
        grid=(n_tiles,),
        in_specs=[
            pl.BlockSpec((TM, hw), lambda i, te, tb, nu: (tb[i], 0)),
            pl.BlockSpec((1, d, f2), lambda i, te, tb, nu: (te[i], 0, 0)),
            pl.BlockSpec((1, 1, f2), lambda i, te, tb, nu: (te[i], 0, 0)),
            pl.BlockSpec((1, fdim, d), lambda i, te, tb, nu: (te[i], 0, 0)),
            pl.BlockSpec((1, 1, d), lambda i, te, tb, nu: (te[i], 0, 0)),
        ],
        out_specs=pl.BlockSpec((TM, hw), lambda i, te, tb, nu: (i, 0)),
    )
    return pl.pallas_call(
        _experts_kernel,
        grid_spec=grid_spec,
        out_shape=jax.ShapeDtypeStruct((n_rows, hw), U32),
        compiler_params=_cparams(("arbitrary",), VMEM_LIMIT),
        name="moe_experts",
    )(tile_expert, tile_block, n_used, xs, w1p, b1p.reshape(e, 1, f2), w2b, b2.reshape(e, 1, d))


def _combine_kernel(pos_ref, ys_ref, gt_ref, x_ref, mod_ref, gf_ref, o_ref, ybuf, sem, *, final):
    d = x_ref.shape[-1]
    half = d // 2
    _row_copies(pos_ref,
                lambda k, t, p: ys_ref.at[pl.ds(p, 1)],
                lambda k, t, p: ybuf.at[k, pl.ds(t, 1)], sem)
    for k in range(TOP_K):
        pltpu.make_async_copy(ys_ref.at[pl.ds(0, TM)], ybuf.at[k], sem).wait()
    acc_lo = jnp.zeros((TM, half), F32)
    acc_hi = jnp.zeros((TM, half), F32)
    for k in range(TOP_K):
        lo, hi = _unpack_bf16_pair(ybuf[k])
        g = gt_ref[:, k:k + 1]
        acc_lo = acc_lo + g * lo
        acc_hi = acc_hi + g * hi
    x = x_ref[0]
    gt2 = mod_ref[0, :, 5 * d:6 * d]
    xo = jnp.concatenate([x[:, 0:half] + gt2[:, 0:half] * acc_lo,
                          x[:, half:d] + gt2[:, half:d] * acc_hi], axis=1)
    if final:
        xo = xo * lax.rsqrt(jnp.mean(xo * xo, axis=-1, keepdims=True) + EPS) * gf_ref[...]
    o_ref[0] = xo


def _combine(pos_flat, ys, gate_t, xmid, mod3, g_final, q0, final):
    b, t, d = xmid.shape
    nt = t // TM
    return pl.pallas_call(
        functools.partial(_combine_kernel, final=final),
        grid=(b, nt),
        in_specs=[
            pl.BlockSpec((TOP_K * TM,), lambda bb, i: (bb * nt + i,), memory_space=pltpu.SMEM),
            pl.BlockSpec(memory_space=pl.ANY),
            pl.BlockSpec((TM, LANES), lambda bb, i: (bb * nt + i, 0)),
            pl.BlockSpec((1, TM, d), lambda bb, i: (bb, i, 0)),
            pl.BlockSpec((1, 1, mod3.shape[-1]), lambda bb, i: (jnp.where(i + q0 == 0, b, bb), 0, 0)),
            pl.BlockSpec((1, d), lambda bb, i: (0, 0)),
        ],
        out_specs=pl.BlockSpec((1, TM, d), lambda bb, i: (bb, i, 0)),
        out_shape=jax.ShapeDtypeStruct((b, t, d), F32),
        scratch_shapes=[pltpu.VMEM((TOP_K, TM, d // 2), U32), pltpu.SemaphoreType.DMA(())],
        compiler_params=_cparams(("arbitrary", "arbitrary"), VMEM_LIMIT),
        name="moe_combine",
    )(pos_flat, ys, gate_t, xmid, mod3, g_final)


def _routing_tables(top_e, rank, counts, n_tiles):
    cnt = counts[:, 0].astype(I32)
    tiles_e = (cnt + TM - 1) // TM
    tile_end = jnp.cumsum(tiles_e)
    row_start = (tile_end - tiles_e) * TM
    pos = row_start[top_e[0:TOP_K]] + rank[0:TOP_K]
    ntok = pos.shape[1]
    pos_flat = pos.reshape(TOP_K, ntok // TM, TM).transpose(1, 0, 2).reshape(-1)
    n_used = tile_end[-1]
    ti = jnp.minimum(jnp.arange(n_tiles, dtype=I32), n_used - 1)
    tile_expert = jnp.minimum(jnp.searchsorted(tile_end, ti, side="right"), N_EXPERTS - 1).astype(I32)
    return pos_flat, tile_expert, ti, n_used.reshape(1).astype(I32)


def _rope_tables(ctx_len, seq, head_dim, lanes_per_group):
    t = np.arange(seq)
    row, col = t // GRID_W, t % GRID_W
    quarter = head_dim // 4
    inv = ROPE_THETA ** (-np.arange(0, head_dim // 2, 2, dtype=np.float64) / (head_dim // 2))
    lane = np.arange(LANES) % lanes_per_group
    is_col = (lane // (head_dim // 2)) % 2 == 1
    w = lane % (head_dim // 2)
    freq = inv[w % quarter]
    pos = np.where(is_col[None, :], col[:, None], row[:, None]).astype(np.float64)
    ang = pos * freq[None, :]
    sign = np.where(w < quarter, -1.0, 1.0)[None, :]
    cos = np.concatenate([np.ones((ctx_len, LANES)), np.cos(ang)], axis=0)
    sin = np.concatenate([np.zeros((ctx_len, LANES)), np.sin(ang) * sign], axis=0)
    return jnp.asarray(cos, F32), jnp.asarray(sin, F32)


def _in_column_order():
    cols = list(range(A_Q + 2 * A_KV))
    for base in (A_Q + 2 * A_KV, A_Q + 2 * A_KV + B_QK):
        for h in range(B_HEADS):
            for c in range(2):
                start = base + c * B_HEADS * B_QK_DIM + h * B_QK_DIM
                cols.extend(range(start, start + B_QK_DIM))
    n = A_Q + 2 * A_KV + 2 * B_QK
    cols.extend(range(n, n + B_V + C_WIDTH))
    return np.asarray(cols, np.int32)


def kernel(x, c, ctx, c_ctx, w_mod, b_mod, g_norm1, g_norm2, w_in, g_qnorm, g_knorm, lambda_q1, lambda_k1,
           lambda_q2, lambda_k2, g_subln, w_pool, pool_scale, w_out, w_router, b_router, w_expert_in,
           b_expert_in, w_expert_out, b_expert_out, g_final):
    b, seq, d = x.shape
    ctx_len = ctx.shape[1]
    depth = w_mod.shape[0]
    assert ctx_len == TM and seq % TM == 0

    xa = jnp.concatenate([ctx, x], axis=1)
    cc = jnp.concatenate([c, c_ctx[None, :], jnp.zeros((8 - b - 1, d), F32)], axis=0)
    mods = _modulation(cc, w_mod, b_mod)
    rope_a = _rope_tables(ctx_len, seq, HEAD_DIM, HEAD_DIM)
    rope_b = _rope_tables(ctx_len, seq, B_QK_DIM, B_QK_DIM)
    col_order = _in_column_order()
    e, _, f2 = w_expert_in.shape[1:]

    for l in range(depth):
        last = l == depth - 1
        q0 = 1 if last else 0
        lam_init = 0.8 - 0.6 * math.exp(-0.3 * l)
        mod3 = mods[l].reshape(8, 1, 6 * d)
        w_in_p = w_in[l][:, col_order].astype(BF16)
        qa, ka, vat, qb, kb, vbt, uc = _in_projection(
            xa, mod3, g_norm1[l][None], w_in_p, g_qnorm[l][None], g_knorm[l][None], rope_a + rope_b)
        oa = _attention_a(qa, ka, vat, q0)
        lam_vecs = jnp.stack([lambda_q1[l], lambda_k1[l], lambda_q2[l], lambda_k2[l]])
        ob = _attention_b(qb, kb, vbt, lam_vecs, g_subln[l][:, None], q0, lam_init)
        oc = _pooling(uc, w_pool[l].astype(BF16), pool_scale[l][None], q0, ctx_len)

        wr = jnp.pad(w_router[l], ((0, 0), (0, LANES - e)))
        wr_hi = wr.astype(BF16)
        wr2 = jnp.concatenate([wr_hi, (wr - wr_hi.astype(F32)).astype(BF16)], axis=1)
        br = jnp.pad(b_router[l], (0, LANES - e))[None]
        xmid, fpk, top_e, rank, gate_t, counts = _post_attention(
            oa, ob, oc, xa, mod3, g_norm2[l][None], w_out[l].astype(BF16), wr2, br, q0)

        ntok = top_e.shape[1]
        n_tiles = (TOP_K * ntok) // TM + e
        pos_flat, tile_expert, tile_block, n_used = _routing_tables(top_e, rank, counts, n_tiles)
        xs = _dispatch(pos_flat, fpk, n_tiles * TM)
        w1 = w_expert_in[l]
        w1p = jnp.concatenate([w1[..., 0::2], w1[..., 1::2]], axis=-1).astype(BF16)
        b1 = b_expert_in[l]
        b1p = jnp.concatenate([b1[..., 0::2], b1[..., 1::2]], axis=-1)
        ys = _experts(tile_expert, tile_block, n_used, xs, w1p, b1p,
                      w_expert_out[l].astype(BF16), b_expert_out[l])
        xa = _combine(pos_flat, ys, gate_t, xmid, mod3, g_final[None], q0, last)
    return xa
```

```python
import functools
import math

import numpy as np
import jax
import jax.numpy as jnp
from jax import lax
from jax.experimental import pallas as pl
from jax.experimental.pallas import tpu as pltpu

F32 = jnp.float32
BF16 = jnp.bfloat16
U32 = jnp.uint32
I32 = jnp.int32

HEAD_DIM = 128
A_HEADS = 8
A_KV_HEADS = 2
GQA_GROUP = A_HEADS // A_KV_HEADS
B_HEADS = 4
B_QK_DIM = 64
POOL_WINDOWS = (2, 4, 8, 16)
POOL_GROUP = 128
C_WIDTH = len(POOL_WINDOWS) * POOL_GROUP
A_Q = A_HEADS * HEAD_DIM
A_KV = A_KV_HEADS * HEAD_DIM
B_QK = B_HEADS * 2 * B_QK_DIM
B_V = B_HEADS * 2 * B_QK_DIM
N_EXPERTS = 32
TOP_K = 4
SWIGLU_LIMIT = 7.0
SWIGLU_ALPHA = 1.702
ROPE_THETA = 10000.0
GRID_W = 64
EPS = 1e-6

TM = 256
KV_TILE = 512
POOL_HALO = 8
LANES = 128
VMEM_LIMIT = 56 * 1024 * 1024
LOG2E = math.log2(math.e)
NEG_BIG = -1e30


def _cparams(sem, vmem=None):
    return pltpu.CompilerParams(dimension_semantics=sem, vmem_limit_bytes=vmem)


def _sigmoid(z):
    return 1.0 / (1.0 + jnp.exp(-z))


def _pack_bf16_pair(lo, hi):
    lo_b = lax.bitcast_convert_type(lo.astype(BF16).astype(F32), U32)
    hi_b = lax.bitcast_convert_type(hi.astype(BF16).astype(F32), U32)
    return (hi_b & jnp.uint32(0xFFFF0000)) | (lo_b >> 16)


def _unpack_bf16_pair(p):
    lo = lax.bitcast_convert_type(p << 16, F32)
    hi = lax.bitcast_convert_type(p & jnp.uint32(0xFFFF0000), F32)
    return lo, hi


def _mod_kernel(cc_ref, w_ref, b_ref, o_ref):
    cc = cc_ref[...]
    a = (cc * _sigmoid(cc)).astype(BF16)
    o_ref[0] = jnp.dot(a, w_ref[0].astype(BF16), preferred_element_type=F32) + b_ref[0]


def _modulation(cc, w_mod, b_mod):
    depth, d, n = w_mod.shape
    tn = 1536
    return pl.pallas_call(
        _mod_kernel,
        grid=(depth, n // tn),
        in_specs=[
            pl.BlockSpec((8, d), lambda l, j: (0, 0)),
            pl.BlockSpec((1, d, tn), lambda l, j: (l, 0, j)),
            pl.BlockSpec((1, 1, tn), lambda l, j: (l, 0, j)),
        ],
        out_specs=pl.BlockSpec((1, 8, tn), lambda l, j: (l, 0, j)),
        out_shape=jax.ShapeDtypeStruct((depth, 8, n), F32),
        compiler_params=_cparams(("parallel", "parallel"), VMEM_LIMIT),
        name="modulation",
    )(cc, w_mod, b_mod.reshape(depth, 1, n))


def _rope(y, cos, sin_signed, half):
    width = y.shape[-1]
    lane = lax.broadcasted_iota(I32, y.shape, 1)
    partner = jnp.where(lane % (2 * half) < half,
                        pltpu.roll(y, width - half, 1), pltpu.roll(y, half, 1))
    return y * cos + partner * sin_signed


def _head_rms(y, g):
    return y * lax.rsqrt(jnp.mean(y * y, axis=-1, keepdims=True) + EPS) * g


def _proj_kernel(x_ref, mod_ref, g1_ref, w_ref, gq_ref, gk_ref, ca_ref, sa_ref, cb_ref, sb_ref,
                 qa_ref, ka_ref, vat_ref, qb_ref, kb_ref, vbt_ref, uc_ref):
    d = x_ref.shape[-1]
    x = x_ref[0]
    xn = x * lax.rsqrt(jnp.mean(x * x, axis=-1, keepdims=True) + EPS) * g1_ref[...]
    sh = mod_ref[0, :, 0:d]
    sc = mod_ref[0, :, d:2 * d]
    h = (xn * (1.0 + sc) + sh).astype(BF16)
    ca, sa, cb, sb = ca_ref[...], sa_ref[...], cb_ref[...], sb_ref[...]

    c0 = 0
    pq = jnp.dot(h, w_ref[:, c0:c0 + A_Q], preferred_element_type=F32)
    for hd in range(A_HEADS):
        y = _head_rms(pq[:, hd * HEAD_DIM:(hd + 1) * HEAD_DIM], gq_ref[...])
        qa_ref[0, :, hd * HEAD_DIM:(hd + 1) * HEAD_DIM] = _rope(y, ca, sa, HEAD_DIM // 4).astype(BF16)
    c0 += A_Q
    pkv = jnp.dot(h, w_ref[:, c0:c0 + 2 * A_KV], preferred_element_type=F32)
    for hd in range(A_KV_HEADS):
        y = _head_rms(pkv[:, hd * HEAD_DIM:(hd + 1) * HEAD_DIM], gk_ref[...])
        ka_ref[0, :, hd * HEAD_DIM:(hd + 1) * HEAD_DIM] = _rope(y, ca, sa, HEAD_DIM // 4).astype(BF16)
    vat_ref[0] = pkv[:, A_KV:2 * A_KV].T.astype(BF16)
    c0 += 2 * A_KV
    pb = jnp.dot(h, w_ref[:, c0:c0 + 2 * B_QK], preferred_element_type=F32)
    lane = lax.broadcasted_iota(I32, (x.shape[0], LANES), 1)

    def head_pair(base, hd):
        v0 = pb[:, base + (hd // 2) * LANES:base + (hd // 2 + 1) * LANES]
        v1 = pb[:, base + B_QK // 2 + (hd // 2) * LANES:base + B_QK // 2 + (hd // 2 + 1) * LANES]
        if hd % 2 == 0:
            return jnp.where(lane < B_QK_DIM, v0, pltpu.roll(v1, B_QK_DIM, 1))
        return jnp.where(lane < B_QK_DIM, pltpu.roll(v0, B_QK_DIM, 1), v1)

    for hd in range(B_HEADS):
        sl = slice(hd * LANES, (hd + 1) * LANES)
        qb_ref[0, :, sl] = _rope(head_pair(0, hd), cb, sb, B_QK_DIM // 4).astype(BF16)
        kb_ref[0, :, sl] = _rope(head_pair(B_QK, hd), cb, sb, B_QK_DIM // 4).astype(BF16)
    c0 += 2 * B_QK
    pvu = jnp.dot(h, w_ref[:, c0:c0 + B_V + C_WIDTH], preferred_element_type=F32)
    vbt_ref[0] = pvu[:, 0:B_V].T.astype(BF16)
    uc_ref[0] = pvu[:, B_V:B_V + C_WIDTH]


def _in_projection(xa, mod3, g1, w_in_p, gq, gk, rope):
    b, t, d = xa.shape
    nt = t // TM
    n_in = w_in_p.shape[1]
    tok = lambda w: pl.BlockSpec((1, TM, w), lambda bb, i: (bb, i, 0))
    tok_t = lambda w: pl.BlockSpec((1, w, TM), lambda bb, i: (bb, 0, i))
    const2 = lambda r, c: pl.BlockSpec((r, c), lambda bb, i: (0, 0))
    tab = pl.BlockSpec((TM, LANES), lambda bb, i: (i, 0))
    return pl.pallas_call(
        _proj_kernel,
        grid=(b, nt),
        in_specs=[
            tok(d),
            pl.BlockSpec((1, 1, mod3.shape[-1]), lambda bb, i: (jnp.where(i == 0, b, bb), 0, 0)),
            const2(1, d), const2(d, n_in), const2(1, HEAD_DIM), const2(1, HEAD_DIM),
            tab, tab, tab, tab,
        ],
        out_specs=[tok(A_Q), tok(A_KV), tok_t(A_KV), tok(B_QK), tok(B_QK), tok_t(B_V), tok(C_WIDTH)],
        out_shape=[
            jax.ShapeDtypeStruct((b, t, A_Q), BF16),
            jax.ShapeDtypeStruct((b, t, A_KV), BF16),
            jax.ShapeDtypeStruct((b, A_KV, t), BF16),
            jax.ShapeDtypeStruct((b, t, B_QK), BF16),
            jax.ShapeDtypeStruct((b, t, B_QK), BF16),
            jax.ShapeDtypeStruct((b, B_V, t), BF16),
            jax.ShapeDtypeStruct((b, t, C_WIDTH), F32),
        ],
        compiler_params=_cparams(("parallel", "parallel"), VMEM_LIMIT),
        name="in_projection",
    )(xa, mod3, g1, w_in_p, gq, gk, *rope)


def _flash_t(q_stacks, k_ref, vt_ref, n_kv, scale):
    mq = q_stacks[0].shape[0]
    c = scale * LOG2E

    def step(carry, off, size):
        ss = [lax.dot_general(k_ref[0, pl.ds(off, size), h * LANES:(h + 1) * LANES], q,
                              (((1,), (1,)), ((), ())), preferred_element_type=F32)
              for h, q in enumerate(q_stacks)]
        out = []
        for h, (s, (m, l, acc)) in enumerate(zip(ss, carry)):
            m_new = jnp.maximum(m, jnp.max(s, axis=0, keepdims=True))
            alpha = jnp.exp2((m - m_new) * c)
            p = jnp.exp2((s - m_new) * c)
            l = alpha * l + jnp.sum(p, axis=0, keepdims=True)
            vt = vt_ref[0, h * HEAD_DIM:(h + 1) * HEAD_DIM, pl.ds(off, size)]
            acc = alpha * acc + jnp.dot(vt, p.astype(BF16), preferred_element_type=F32)
            out.append((m_new, l, acc))
        return out

    init = [(jnp.full((1, mq), NEG_BIG, F32), jnp.zeros((1, mq), F32), jnp.zeros((HEAD_DIM, mq), F32))
            for _ in q_stacks]
    carry = step(init, 0, TM)
    carry = lax.fori_loop(
        0, n_kv, lambda j, cr: step(cr, pl.multiple_of(TM + j * KV_TILE, KV_TILE // 2), KV_TILE), carry)
    return [(acc, l) for _, l, acc in carry]


def _attn_a_kernel(q_ref, k_ref, vt_ref, o_ref, *, q0, n_kv_full):
    qi = pl.program_id(1) + q0
    n_kv = jnp.where(qi == 0, 0, n_kv_full)
    q = q_ref[0]
    gw = GQA_GROUP * HEAD_DIM
    q_stacks = [jnp.concatenate([q[:, h * gw + g * HEAD_DIM:h * gw + (g + 1) * HEAD_DIM]
                                 for g in range(GQA_GROUP)], axis=0) for h in range(A_KV_HEADS)]
    res = _flash_t(q_stacks, k_ref, vt_ref, n_kv, 1.0 / math.sqrt(HEAD_DIM))
    for h, (acc, l) in enumerate(res):
        o = acc / l
        for g in range(GQA_GROUP):
            c0 = h * gw + g * HEAD_DIM
            o_ref[0, :, c0:c0 + HEAD_DIM] = o[:, g * TM:(g + 1) * TM].T.astype(BF16)


def _attn_b_kernel(q_ref, k_ref, vt_ref, lam_ref, gs_ref, o_ref, *, q0, n_kv_full, lam_init):
    qi = pl.program_id(1) + q0
    n_kv = jnp.where(qi == 0, 0, n_kv_full)
    lane = lax.broadcasted_iota(I32, (TM, LANES), 1)
    zero = jnp.zeros((TM, LANES), BF16)
    q_stacks = []
    for h in range(B_HEADS):
        q = q_ref[0, :, h * LANES:(h + 1) * LANES]
        q_stacks.append(jnp.concatenate(
            [jnp.where(lane < B_QK_DIM, q, zero), jnp.where(lane >= B_QK_DIM, q, zero)], axis=0))
    res = _flash_t(q_stacks, k_ref, vt_ref, n_kv, 1.0 / math.sqrt(B_QK_DIM))
    lv = lam_ref[...]
    lam = (jnp.exp(jnp.sum(lv[0:1] * lv[1:2], axis=-1, keepdims=True))
           - jnp.exp(jnp.sum(lv[2:3] * lv[3:4], axis=-1, keepdims=True)) + lam_init)
    for h, (acc, l) in enumerate(res):
        o = acc / l
        od = o[:, 0:TM] - lam * o[:, TM:2 * TM]
        r = lax.rsqrt(jnp.mean(od * od, axis=0, keepdims=True) + EPS)
        y = od * r * gs_ref[...] * (1.0 - lam_init)
        o_ref[0, :, h * LANES:(h + 1) * LANES] = y.T.astype(BF16)


def _attention_a(qa, ka, vat, q0):
    b, t, _ = qa.shape
    nt = t // TM
    return pl.pallas_call(
        functools.partial(_attn_a_kernel, q0=q0, n_kv_full=(t - TM) // KV_TILE),
        grid=(b, nt - q0),
        in_specs=[
            pl.BlockSpec((1, TM, A_Q), lambda bb, i: (bb, i + q0, 0)),
            pl.BlockSpec((1, t, A_KV), lambda bb, i: (bb, 0, 0)),
            pl.BlockSpec((1, A_KV, t), lambda bb, i: (bb, 0, 0)),
        ],
        out_specs=pl.BlockSpec((1, TM, A_Q), lambda bb, i: (bb, i, 0)),
        out_shape=jax.ShapeDtypeStruct((b, t - q0 * TM, A_Q), BF16),
        compiler_params=_cparams(("parallel", "parallel"), VMEM_LIMIT),
        name="attention_gqa",
    )(qa, ka, vat)


def _attention_b(qb, kb, vbt, lam_vecs, g_sub_col, q0, lam_init):
    b, t, _ = qb.shape
    nt = t // TM
    return pl.pallas_call(
        functools.partial(_attn_b_kernel, q0=q0, n_kv_full=(t - TM) // KV_TILE, lam_init=lam_init),
        grid=(b, nt - q0),
        in_specs=[
            pl.BlockSpec((1, TM, B_QK), lambda bb, i: (bb, i + q0, 0)),
            pl.BlockSpec((1, t, B_QK), lambda bb, i: (bb, 0, 0)),
            pl.BlockSpec((1, B_V, t), lambda bb, i: (bb, 0, 0)),
            pl.BlockSpec((4, B_QK_DIM), lambda bb, i: (0, 0)),
            pl.BlockSpec((LANES, 1), lambda bb, i: (0, 0)),
        ],
        out_specs=pl.BlockSpec((1, TM, B_V), lambda bb, i: (bb, i, 0)),
        out_shape=jax.ShapeDtypeStruct((b, t - q0 * TM, B_V), BF16),
        compiler_params=_cparams(("parallel", "parallel"), VMEM_LIMIT),
        name="attention_diff",
    )(qb, kb, vbt, lam_vecs, g_sub_col)


def _pool_kernel(prev_ref, cur_ref, next_ref, wp_ref, ps_ref, o_ref, *, q0, ctx_len, total_len):
    i = pl.program_id(1) + q0
    ext = jnp.concatenate([prev_ref[0], cur_ref[0], next_ref[0]], axis=0)
    rows = ext.shape[0]
    seg_lo = jnp.where(i == 0, 0, ctx_len)
    seg_hi = jnp.where(i == 0, ctx_len, total_len)
    grow = i * TM - POOL_HALO + lax.broadcasted_iota(I32, (rows, 1), 0)
    ext = jnp.where((grow >= seg_lo) & (grow < seg_hi), ext, 0.0)
    tpos = grow[POOL_HALO:POOL_HALO + TM] - seg_lo
    seg_len = seg_hi - seg_lo

    def back(a, d):
        return pltpu.roll(a, d, 0)

    def fwd(a, d):
        return pltpu.roll(a, rows - d, 0)

    for gi, w in enumerate(POOL_WINDOWS):
        u = ext[:, gi * POOL_GROUP:(gi + 1) * POOL_GROUP]
        acc = u + back(u, 1)
        span = 2
        while span < w:
            acc = acc + back(acc, span)
            span *= 2
        if w > 2:
            acc = fwd(acc, w // 2 - 1)
        cnt = (jnp.minimum(tpos + w // 2, seg_len) - jnp.maximum(tpos - w // 2, 0)).astype(F32)
        sl = slice(POOL_HALO, POOL_HALO + TM)
        p = (acc[sl] / cnt - u[sl]).astype(BF16)
        y = jnp.dot(p, wp_ref[gi], preferred_element_type=F32)
        o_ref[0, :, gi * POOL_GROUP:(gi + 1) * POOL_GROUP] = (
            y * ps_ref[:, gi * POOL_GROUP:(gi + 1) * POOL_GROUP]).astype(BF16)


def _pooling(uc, w_pool_b, pool_scale, q0, ctx_len):
    b, t, cw = uc.shape
    nt = t // TM
    per = TM // POOL_HALO
    last = t // POOL_HALO - 1
    return pl.pallas_call(
        functools.partial(_pool_kernel, q0=q0, ctx_len=ctx_len, total_len=t),
        grid=(b, nt - q0),
        in_specs=[
            pl.BlockSpec((1, POOL_HALO, cw), lambda bb, i: (bb, jnp.maximum((i + q0) * per - 1, 0), 0)),
            pl.BlockSpec((1, TM, cw), lambda bb, i: (bb, i + q0, 0)),
            pl.BlockSpec((1, POOL_HALO, cw), lambda bb, i: (bb, jnp.minimum((i + q0 + 1) * per, last), 0)),
            pl.BlockSpec(w_pool_b.shape, lambda bb, i: (0, 0, 0)),
            pl.BlockSpec((1, cw), lambda bb, i: (0, 0)),
        ],
        out_specs=pl.BlockSpec((1, TM, cw), lambda bb, i: (bb, i, 0)),
        out_shape=jax.ShapeDtypeStruct((b, t - q0 * TM, cw), BF16),
        compiler_params=_cparams(("parallel", "parallel")),
        name="pooling",
    )(uc, uc, uc, w_pool_b, pool_scale)


def _post_kernel(oa_ref, ob_ref, oc_ref, x_ref, mod_ref, g2_ref, wo_ref, wr_ref, br_ref,
                 xmid_ref, fpk_ref, te_ref, rk_ref, gt_ref, cnt_ref, carry_ref):
    d = x_ref.shape[-1]
    first = (pl.program_id(0) == 0) & (pl.program_id(1) == 0)

    @pl.when(first)
    def _():
        carry_ref[...] = jnp.zeros_like(carry_ref)

    m = (jnp.dot(oa_ref[0], wo_ref[0:A_Q], preferred_element_type=F32)
         + jnp.dot(ob_ref[0], wo_ref[A_Q:A_Q + B_V], preferred_element_type=F32)
         + jnp.dot(oc_ref[0], wo_ref[A_Q + B_V:A_Q + B_V + C_WIDTH], preferred_element_type=F32))
    x = x_ref[0] + mod_ref[0, :, 2 * d:3 * d] * m
    xmid_ref[0] = x
    xn = x * lax.rsqrt(jnp.mean(x * x, axis=-1, keepdims=True) + EPS) * g2_ref[...]
    f = xn * (1.0 + mod_ref[0, :, 4 * d:5 * d]) + mod_ref[0, :, 3 * d:4 * d]
    half = d // 2
    fpk_ref[0] = _pack_bf16_pair(f[:, 0:half], f[:, half:d])

    f_hi = f.astype(BF16)
    f_lo = (f - f_hi.astype(F32)).astype(BF16)
    lg2 = jnp.dot(f_hi, wr_ref[...], preferred_element_type=F32)
    lg1 = jnp.dot(f_lo, wr_ref[:, 0:LANES], preferred_element_type=F32)
    logits = lg2[:, 0:LANES] + lg2[:, LANES:2 * LANES] + lg1 + br_ref[...]
    work = logits.T[0:N_EXPERTS]

    e_id = lax.broadcasted_iota(I32, work.shape, 0).astype(F32)
    vals, idxs, hots = [], [], []
    for _ in range(TOP_K):
        mx = jnp.max(work, axis=0, keepdims=True)
        idx = jnp.min(jnp.where(work == mx, e_id, float(N_EXPERTS)), axis=0, keepdims=True)
        hot = e_id == idx
        vals.append(mx)
        idxs.append(idx)
        hots.append(hot)
        work = jnp.where(hot, -jnp.inf, work)
    ex = [jnp.exp(v - vals[0]) for v in vals]
    den = ex[0] + ex[1] + ex[2] + ex[3]
    gates = [e / den for e in ex]

    msel = jnp.zeros(work.shape, F32)
    for hot in hots:
        msel = msel + jnp.where(hot, 1.0, 0.0)
    r_i = lax.broadcasted_iota(I32, (TM, TM), 0)
    c_i = lax.broadcasted_iota(I32, (TM, TM), 1)
    tri = jnp.where(r_i < c_i, 1.0, 0.0).astype(BF16)
    carry = carry_ref[:, 0:1]
    rank_full = jnp.dot(msel.astype(BF16), tri, preferred_element_type=F32) + carry
    ranks = [jnp.sum(jnp.where(hot, rank_full, 0.0), axis=0, keepdims=True) for hot in hots]
    new_carry = carry + jnp.sum(msel, axis=1, keepdims=True)
    carry_ref[...] = jnp.broadcast_to(new_carry, carry_ref.shape)
    cnt_ref[...] = jnp.broadcast_to(new_carry, cnt_ref.shape)

    row8 = lax.broadcasted_iota(I32, (8, TM), 0)

    def rows8(vs):
        out = jnp.zeros((8, TM), F32)
        for k, v in enumerate(vs):
            out = jnp.where(row8 == k, v, out)
        return out

    te_ref[...] = rows8(idxs).astype(I32)
    rk_ref[...] = rows8(ranks).astype(I32)
    g128 = jnp.concatenate([rows8(gates), jnp.zeros((LANES - 8, TM), F32)], axis=0)
    gt_ref[...] = g128.T


def _post_attention(oa, ob, oc, xa, mod3, g2, w_out_b, wr2, br, q0):
    b, t, d = xa.shape
    nt = t // TM - q0
    ntok = b * nt * TM
    tok = lambda w: pl.BlockSpec((1, TM, w), lambda bb, i: (bb, i, 0))
    const2 = lambda r, c: pl.BlockSpec((r, c), lambda bb, i: (0, 0))
    flat = lambda r: pl.BlockSpec((r, TM), lambda bb, i: (0, bb * nt + i))
    return pl.pallas_call(
        _post_kernel,
        grid=(b, nt),
        in_specs=[
            tok(A_Q), tok(B_V), tok(C_WIDTH),
            pl.BlockSpec((1, TM, d), lambda bb, i: (bb, i + q0, 0)),
            pl.BlockSpec((1, 1, mod3.shape[-1]), lambda bb, i: (jnp.where(i + q0 == 0, b, bb), 0, 0)),
            const2(1, d), const2(d, d), const2(d, 2 * LANES), const2(1, LANES),
        ],
        out_specs=[
            tok(d), tok(d // 2), flat(8), flat(8),
            pl.BlockSpec((TM, LANES), lambda bb, i: (bb * nt + i, 0)),
            const2(N_EXPERTS, LANES),
        ],
        out_shape=[
            jax.ShapeDtypeStruct((b, nt * TM, d), F32),
            jax.ShapeDtypeStruct((b, nt * TM, d // 2), U32),
            jax.ShapeDtypeStruct((8, ntok), I32),
            jax.ShapeDtypeStruct((8, ntok), I32),
            jax.ShapeDtypeStruct((ntok, LANES), F32),
            jax.ShapeDtypeStruct((N_EXPERTS, LANES), F32),
        ],
        scratch_shapes=[pltpu.VMEM((N_EXPERTS, LANES), F32)],
        compiler_params=_cparams(("arbitrary", "arbitrary"), VMEM_LIMIT),
        name="out_projection_router",
    )(oa, ob, oc, xa, mod3, g2, w_out_b, wr2, br)


def _row_copies(pos_ref, src_row, dst_row, sem):
    def issue(t, carry):
        for k in range(TOP_K):
            p = pos_ref[k * TM + t]
            pltpu.make_async_copy(src_row(k, t, p), dst_row(k, t, p), sem).start()
        return carry

    lax.fori_loop(0, TM, issue, 0, unroll=8)


def _dispatch_kernel(pos_ref, tend_ref, f_ref, xs_ref, zbuf, sem, zsem):
    first = (pl.program_id(0) == 0) & (pl.program_id(1) == 0)

    @pl.when(first)
    def _():
        zbuf[...] = jnp.zeros_like(zbuf)

        def last_tile_copy(e):
            return pltpu.make_async_copy(zbuf, xs_ref.at[pl.ds((tend_ref[e] - 1) * TM, TM)], zsem)

        def has_tiles(e):
            return tend_ref[e] > (tend_ref[e - 1] if e else 0)

        def tail_copy(i):
            return pltpu.make_async_copy(zbuf, xs_ref.at[pl.ds(i * TM, TM)], zsem)

        n_used, n_tiles = tend_ref[N_EXPERTS - 1], xs_ref.shape[0] // TM
        for e in range(N_EXPERTS):
            pl.when(has_tiles(e))(lambda e=e: last_tile_copy(e).start())
        lax.fori_loop(n_used, n_tiles, lambda i, c: (tail_copy(i).start(), c)[1], 0)
        for e in range(N_EXPERTS):
            pl.when(has_tiles(e))(lambda e=e: last_tile_copy(e).wait())
        lax.fori_loop(n_used, n_tiles, lambda i, c: (tail_copy(i).wait(), c)[1], 0)

    _row_copies(pos_ref,
                lambda k, t, p: f_ref.at[0, pl.ds(t, 1)],
                lambda k, t, p: xs_ref.at[pl.ds(p, 1)], sem)
    for k in range(TOP_K):
        pltpu.make_async_copy(f_ref.at[0], xs_ref.at[pl.ds(0, TM)], sem).wait()


def _dispatch(pos_flat, tile_end, fpk, n_rows):
    b, t, hw = fpk.shape
    nt = t // TM
    return pl.pallas_call(
        _dispatch_kernel,
        grid=(b, nt),
        in_specs=[
            pl.BlockSpec((TOP_K * TM,), lambda bb, i: (bb * nt + i,), memory_space=pltpu.SMEM),
            pl.BlockSpec(memory_space=pltpu.SMEM),
            pl.BlockSpec((1, TM, hw), lambda bb, i: (bb, i, 0)),
        ],
        out_specs=pl.BlockSpec(memory_space=pl.ANY),
        out_shape=jax.ShapeDtypeStruct((n_rows, hw), U32),
        scratch_shapes=[pltpu.VMEM((TM, hw), U32), pltpu.SemaphoreType.DMA(()), pltpu.SemaphoreType.DMA(())],
        compiler_params=_cparams(("arbitrary", "arbitrary")),
        name="moe_dispatch",
    )(pos_flat, tile_end, fpk)


def _deinterleave_kernel(w_ref, p_ref, o_ref):
    fdim = w_ref.shape[2] // 2
    for j in range(fdim // LANES):
        chunk = w_ref[0, :, 2 * LANES * j:2 * LANES * (j + 1)].astype(BF16)
        r = jnp.dot(chunk, p_ref[...], preferred_element_type=F32)
        o_ref[0, :, LANES * j:LANES * (j + 1)] = r[:, 0:LANES].astype(BF16)
        o_ref[0, :, fdim + LANES * j:fdim + LANES * (j + 1)] = r[:, LANES:2 * LANES].astype(BF16)


def _deinterleave_weights(w, perm):
    n, d, f2 = w.shape
    tk = 512
    return pl.pallas_call(
        _deinterleave_kernel,
        grid=(n, d // tk),
        in_specs=[
            pl.BlockSpec((1, tk, f2), lambda e, j: (e, j, 0)),
            pl.BlockSpec(perm.shape, lambda e, j: (0, 0)),
        ],
        out_specs=pl.BlockSpec((1, tk, f2), lambda e, j: (e, j, 0)),
        out_shape=jax.ShapeDtypeStruct((n, d, f2), BF16),
        compiler_params=_cparams(("parallel", "parallel")),
        name="expert_weight_layout",
    )(w, perm)


def _experts_kernel(te_ref, tb_ref, nu_ref, x_ref, w1_ref, b1_ref, w2_ref, b2_ref, y_ref, w2b_ref):
    del tb_ref
    fdim = w2_ref.shape[1]
    i = pl.program_id(0)

    @pl.when(i >= nu_ref[0])
    def _():
        y_ref[...] = jnp.zeros_like(y_ref)

    @pl.when((i < nu_ref[0]) & ((i == 0) | (te_ref[i] != te_ref[jnp.maximum(i - 1, 0)])))
    def _():
        w2b_ref[...] = w2_ref[0].astype(BF16)

    @pl.when(i < nu_ref[0])
    def _():
        lo, hi = _unpack_bf16_pair(x_ref[...])
        half = lo.shape[1]
        u = (jnp.dot(lo.astype(BF16), w1_ref[0, 0:half, :], preferred_element_type=F32)
             + jnp.dot(hi.astype(BF16), w1_ref[0, half:2 * half, :], preferred_element_type=F32)
             + b1_ref[0])
        glu = jnp.minimum(u[:, 0:fdim], SWIGLU_LIMIT)
        lin = jnp.clip(u[:, fdim:2 * fdim], -SWIGLU_LIMIT, SWIGLU_LIMIT)
        a = glu * _sigmoid(SWIGLU_ALPHA * glu) * (lin + 1.0)
        y = jnp.dot(a.astype(BF16), w2b_ref[...], preferred_element_type=F32) + b2_ref[0]
        y_ref[...] = _pack_bf16_pair(y[:, 0:half], y[:, half:2 * half])


def _experts(tile_expert, tile_block, n_used, xs, w1p, b1p, w2, b2):
    n_rows, hw = xs.shape
    n_tiles = n_rows // TM
    e, d, f2 = w1p.shape
    fdim = f2 // 2
    grid_spec = pltpu.PrefetchScalarGridSpec(
        num_scalar_prefetch=3,
        grid=(n_tiles,),
        in_specs=[
            pl.BlockSpec((TM, hw), lambda i, te, tb, nu: (tb[i], 0)),
            pl.BlockSpec((1, d, f2), lambda i, te, tb, nu: (te[i], 0, 0)),
            pl.BlockSpec((1, 1, f2), lambda i, te, tb, nu: (te[i], 0, 0)),
            pl.BlockSpec((1, fdim, d), lambda i, te, tb, nu: (te[i], 0, 0)),
            pl.BlockSpec((1, 1, d), lambda i, te, tb, nu: (te[i], 0, 0)),
        ],
        out_specs=pl.BlockSpec((TM, hw), lambda i, te, tb, nu: (i, 0)),
        scratch_shapes=[pltpu.VMEM((fdim, d), BF16)],
    )
    return pl.pallas_call(
        _experts_kernel,
        grid_spec=grid_spec,
        out_shape=jax.ShapeDtypeStruct((n_rows, hw), U32),
        compiler_params=_cparams(("arbitrary",), VMEM_LIMIT),
        name="moe_experts",
    )(tile_expert, tile_block, n_used, xs, w1p, b1p.reshape(e, 1, f2), w2, b2.reshape(e, 1, d))


def _combine_kernel(pos_ref, ys_ref, gt_ref, x_ref, mod_ref, gf_ref, o_ref, ybuf, sem, *, final):
    d = x_ref.shape[-1]
    half = d // 2
    _row_copies(pos_ref,
                lambda k, t, p: ys_ref.at[pl.ds(p, 1)],
                lambda k, t, p: ybuf.at[k, pl.ds(t, 1)], sem)
    for k in range(TOP_K):
        pltpu.make_async_copy(ys_ref.at[pl.ds(0, TM)], ybuf.at[k], sem).wait()
    acc_lo = jnp.zeros((TM, half), F32)
    acc_hi = jnp.zeros((TM, half), F32)
    for k in range(TOP_K):
        lo, hi = _unpack_bf16_pair(ybuf[k])
        g = gt_ref[:, k:k + 1]
        acc_lo = acc_lo + g * lo
        acc_hi = acc_hi + g * hi
    x = x_ref[0]
    gt2 = mod_ref[0, :, 5 * d:6 * d]
    xo = jnp.concatenate([x[:, 0:half] + gt2[:, 0:half] * acc_lo,
                          x[:, half:d] + gt2[:, half:d] * acc_hi], axis=1)
    if final:
        xo = xo * lax.rsqrt(jnp.mean(xo * xo, axis=-1, keepdims=True) + EPS) * gf_ref[...]
    o_ref[0] = xo


def _combine(pos_flat, ys, gate_t, xmid, mod3, g_final, q0, final):
    b, t, d = xmid.shape
    nt = t // TM
    return pl.pallas_call(
        functools.partial(_combine_kernel, final=final),
        grid=(b, nt),
        in_specs=[
            pl.BlockSpec((TOP_K * TM,), lambda bb, i: (bb * nt + i,), memory_space=pltpu.SMEM),
            pl.BlockSpec(memory_space=pl.ANY),
            pl.BlockSpec((TM, LANES), lambda bb, i: (bb * nt + i, 0)),
            pl.BlockSpec((1, TM, d), lambda bb, i: (bb, i, 0)),
            pl.BlockSpec((1, 1, mod3.shape[-1]), lambda bb, i: (jnp.where(i + q0 == 0, b, bb), 0, 0)),
            pl.BlockSpec((1, d), lambda bb, i: (0, 0)),
        ],
        out_specs=pl.BlockSpec((1, TM, d), lambda bb, i: (bb, i, 0)),
        out_shape=jax.ShapeDtypeStruct((b, t, d), F32),
        scratch_shapes=[pltpu.VMEM((TOP_K, TM, d // 2), U32), pltpu.SemaphoreType.DMA(())],
        compiler_params=_cparams(("arbitrary", "arbitrary"), VMEM_LIMIT),
        name="moe_combine",
    )(pos_flat, ys, gate_t, xmid, mod3, g_final)


def _routing_tables(top_e, rank, counts, n_tiles):
    cnt = counts[:, 0].astype(I32)
    tiles_e = (cnt + TM - 1) // TM
    tile_end = jnp.cumsum(tiles_e)
    row_start = (tile_end - tiles_e) * TM
    hit = top_e[0:TOP_K, :, None] == jnp.arange(N_EXPERTS, dtype=I32)
    pos = jnp.sum(jnp.where(hit, row_start, 0), axis=-1) + rank[0:TOP_K]
    ntok = pos.shape[1]
    pos_flat = pos.reshape(TOP_K, ntok // TM, TM).transpose(1, 0, 2).reshape(-1)
    n_used = tile_end[-1]
    ti = jnp.minimum(jnp.arange(n_tiles, dtype=I32), n_used - 1)
    tile_expert = jnp.minimum(jnp.sum((tile_end[None, :] <= ti[:, None]).astype(I32), axis=1), N_EXPERTS - 1)
    return pos_flat, tile_expert, ti, n_used.reshape(1).astype(I32), tile_end.astype(I32)


def _rope_tables(ctx_len, seq, head_dim, lanes_per_group):
    t = np.arange(seq)
    row, col = t // GRID_W, t % GRID_W
    quarter = head_dim // 4
    inv = ROPE_THETA ** (-np.arange(0, head_dim // 2, 2, dtype=np.float64) / (head_dim // 2))
    lane = np.arange(LANES) % lanes_per_group
    is_col = (lane // (head_dim // 2)) % 2 == 1
    w = lane % (head_dim // 2)
    freq = inv[w % quarter]
    pos = np.where(is_col[None, :], col[:, None], row[:, None]).astype(np.float64)
    ang = pos * freq[None, :]
    sign = np.where(w < quarter, -1.0, 1.0)[None, :]
    cos = np.concatenate([np.ones((ctx_len, LANES)), np.cos(ang)], axis=0)
    sin = np.concatenate([np.zeros((ctx_len, LANES)), np.sin(ang) * sign], axis=0)
    return jnp.asarray(cos, F32), jnp.asarray(sin, F32)


def _deinterleave_matrix():
    p = np.zeros((2 * LANES, 2 * LANES), np.float32)
    i = np.arange(LANES)
    p[2 * i, i] = 1.0
    p[2 * i + 1, LANES + i] = 1.0
    return jnp.asarray(p, BF16)


def kernel(x, c, ctx, c_ctx, w_mod, b_mod, g_norm1, g_norm2, w_in, g_qnorm, g_knorm, lambda_q1, lambda_k1,
           lambda_q2, lambda_k2, g_subln, w_pool, pool_scale, w_out, w_router, b_router, w_expert_in,
           b_expert_in, w_expert_out, b_expert_out, g_final):
    b, seq, d = x.shape
    ctx_len = ctx.shape[1]
    depth = w_mod.shape[0]
    assert ctx_len == TM and seq % TM == 0

    xa = jnp.concatenate([ctx, x], axis=1)
    cc = jnp.concatenate([c, c_ctx[None, :], jnp.zeros((8 - b - 1, d), F32)], axis=0)
    mods = _modulation(cc, w_mod, b_mod)
    rope_a = _rope_tables(ctx_len, seq, HEAD_DIM, HEAD_DIM)
    rope_b = _rope_tables(ctx_len, seq, B_QK_DIM, B_QK_DIM)
    e, _, f2 = w_expert_in.shape[1:]
    w1p_all = _deinterleave_weights(w_expert_in.reshape(depth * e, d, f2), _deinterleave_matrix())
    w1p_all = w1p_all.reshape(depth, e, d, f2)

    for l in range(depth):
        last = l == depth - 1
        q0 = 1 if last else 0
        lam_init = 0.8 - 0.6 * math.exp(-0.3 * l)
        mod3 = mods[l].reshape(8, 1, 6 * d)
        qa, ka, vat, qb, kb, vbt, uc = _in_projection(
            xa, mod3, g_norm1[l][None], w_in[l].astype(BF16), g_qnorm[l][None], g_knorm[l][None],
            rope_a + rope_b)
        oa = _attention_a(qa, ka, vat, q0)
        lam_vecs = jnp.stack([lambda_q1[l], lambda_k1[l], lambda_q2[l], lambda_k2[l]])
        ob = _attention_b(qb, kb, vbt, lam_vecs, g_subln[l][:, None], q0, lam_init)
        oc = _pooling(uc, w_pool[l].astype(BF16), pool_scale[l][None], q0, ctx_len)

        wr = jnp.pad(w_router[l], ((0, 0), (0, LANES - e)))
        wr_hi = wr.astype(BF16)
        wr2 = jnp.concatenate([wr_hi, (wr - wr_hi.astype(F32)).astype(BF16)], axis=1)
        br = jnp.pad(b_router[l], (0, LANES - e))[None]
        xmid, fpk, top_e, rank, gate_t, counts = _post_attention(
            oa, ob, oc, xa, mod3, g_norm2[l][None], w_out[l].astype(BF16), wr2, br, q0)

        ntok = top_e.shape[1]
        n_tiles = (TOP_K * ntok) // TM + e
        pos_flat, tile_expert, tile_block, n_used, tile_end = _routing_tables(top_e, rank, counts, n_tiles)
        xs = _dispatch(pos_flat, tile_end, fpk, n_tiles * TM)
        b1 = b_expert_in[l]
        b1p = jnp.concatenate([b1[..., 0::2], b1[..., 1::2]], axis=-1)
        ys = _experts(tile_expert, tile_block, n_used, xs, w1p_all[l], b1p,
                      w_expert_out[l], b_expert_out[l])
        xa = _combine(pos_flat, ys, gate_t, xmid, mod3, g_final[None], q0, last)
    return xa
```

```python
import functools
import math

import numpy as np
import jax
import jax.numpy as jnp
from jax import lax
from jax.experimental import pallas as pl
from jax.experimental.pallas import tpu as pltpu

F32 = jnp.float32
BF16 = jnp.bfloat16
U32 = jnp.uint32
I32 = jnp.int32

HEAD_DIM = 128
A_HEADS = 8
A_KV_HEADS = 2
GQA_GROUP = A_HEADS // A_KV_HEADS
B_HEADS = 4
B_QK_DIM = 64
POOL_WINDOWS = (2, 4, 8, 16)
POOL_GROUP = 128
C_WIDTH = len(POOL_WINDOWS) * POOL_GROUP
A_Q = A_HEADS * HEAD_DIM
A_KV = A_KV_HEADS * HEAD_DIM
B_QK = B_HEADS * 2 * B_QK_DIM
B_V = B_HEADS * 2 * B_QK_DIM
N_EXPERTS = 32
TOP_K = 4
SWIGLU_LIMIT = 7.0
SWIGLU_ALPHA = 1.702
ROPE_THETA = 10000.0
GRID_W = 64
EPS = 1e-6

TM = 256
KV_TILE = 512
BF16_ROWS = 16
VT_ROWS = 128 + BF16_ROWS
A_STREAM_HEADS = 2
POOL_HALO = 8
LANES = 128
VMEM_LIMIT = 56 * 1024 * 1024
LOG2E = math.log2(math.e)
NEG_BIG = -1e30


def _cparams(sem, vmem=None):
    return pltpu.CompilerParams(dimension_semantics=sem, vmem_limit_bytes=vmem)


def _sigmoid(z):
    return 1.0 / (1.0 + jnp.exp(-z))


def _mod_kernel(cc_ref, w_ref, b_ref, o_ref):
    cc = cc_ref[...]
    a = (cc * _sigmoid(cc)).astype(BF16)
    o_ref[0] = jnp.dot(a, w_ref[0].astype(BF16), preferred_element_type=F32) + b_ref[0]


def _modulation(cc, w_mod, b_mod):
    depth, d, n = w_mod.shape
    tn = 1536
    return pl.pallas_call(
        _mod_kernel,
        grid=(depth, n // tn),
        in_specs=[
            pl.BlockSpec((8, d), lambda l, j: (0, 0)),
            pl.BlockSpec((1, d, tn), lambda l, j: (l, 0, j)),
            pl.BlockSpec((1, 1, tn), lambda l, j: (l, 0, j)),
        ],
        out_specs=pl.BlockSpec((1, 8, tn), lambda l, j: (l, 0, j)),
        out_shape=jax.ShapeDtypeStruct((depth, 8, n), F32),
        compiler_params=_cparams(("parallel", "parallel"), VMEM_LIMIT),
        name="modulation",
    )(cc, w_mod, b_mod.reshape(depth, 1, n))


def _rope(y, cos, sin_signed, half):
    width = y.shape[-1]
    lane = lax.broadcasted_iota(I32, y.shape, 1)
    partner = jnp.where(lane % (2 * half) < half,
                        pltpu.roll(y, width - half, 1), pltpu.roll(y, half, 1))
    return y * cos + partner * sin_signed


def _head_rms(y, g):
    return y * lax.rsqrt(jnp.mean(y * y, axis=-1, keepdims=True) + EPS) * g


def _proj_kernel(x_ref, mod_ref, g1_ref, w_ref, gq_ref, gk_ref, ca_ref, sa_ref, cb_ref, sb_ref,
                 qa_ref, ka_ref, vat_ref, qb_ref, kb_ref, vbt_ref, uc_ref):
    d = x_ref.shape[-1]
    x = x_ref[0]
    xn = x * lax.rsqrt(jnp.mean(x * x, axis=-1, keepdims=True) + EPS) * g1_ref[...]
    sh = mod_ref[0, :, 0:d]
    sc = mod_ref[0, :, d:2 * d]
    h = (xn * (1.0 + sc) + sh).astype(BF16)
    ca, sa, cb, sb = ca_ref[...], sa_ref[...], cb_ref[...], sb_ref[...]

    qa_scale = LOG2E / math.sqrt(HEAD_DIM)
    qb_scale = LOG2E / math.sqrt(B_QK_DIM)
    ones = jnp.ones((BF16_ROWS, x.shape[0]), BF16)

    def store_vt(ref, hd, v):
        ref[0, hd * VT_ROWS:hd * VT_ROWS + HEAD_DIM, :] = v.T.astype(BF16)
        ref[0, hd * VT_ROWS + HEAD_DIM:(hd + 1) * VT_ROWS, :] = ones

    c0 = 0
    pq = jnp.dot(h, w_ref[:, c0:c0 + A_Q], preferred_element_type=F32)
    for hd in range(A_HEADS):
        y = _head_rms(pq[:, hd * HEAD_DIM:(hd + 1) * HEAD_DIM], gq_ref[...])
        qa_ref[0, :, hd * HEAD_DIM:(hd + 1) * HEAD_DIM] = (
            _rope(y, ca, sa, HEAD_DIM // 4) * qa_scale).astype(BF16)
    c0 += A_Q
    pkv = jnp.dot(h, w_ref[:, c0:c0 + 2 * A_KV], preferred_element_type=F32)
    for hd in range(A_KV_HEADS):
        y = _head_rms(pkv[:, hd * HEAD_DIM:(hd + 1) * HEAD_DIM], gk_ref[...])
        ka_ref[0, :, hd * HEAD_DIM:(hd + 1) * HEAD_DIM] = _rope(y, ca, sa, HEAD_DIM // 4).astype(BF16)
        store_vt(vat_ref, hd, pkv[:, A_KV + hd * HEAD_DIM:A_KV + (hd + 1) * HEAD_DIM])
    c0 += 2 * A_KV
    pb = jnp.dot(h, w_ref[:, c0:c0 + 2 * B_QK], preferred_element_type=F32)
    lane = lax.broadcasted_iota(I32, (x.shape[0], LANES), 1)

    def head_pair(base, hd):
        v0 = pb[:, base + (hd // 2) * LANES:base + (hd // 2 + 1) * LANES]
        v1 = pb[:, base + B_QK // 2 + (hd // 2) * LANES:base + B_QK // 2 + (hd // 2 + 1) * LANES]
        if hd % 2 == 0:
            return jnp.where(lane < B_QK_DIM, v0, pltpu.roll(v1, B_QK_DIM, 1))
        return jnp.where(lane < B_QK_DIM, pltpu.roll(v0, B_QK_DIM, 1), v1)

    for hd in range(B_HEADS):
        sl = slice(hd * LANES, (hd + 1) * LANES)
        qb_ref[0, :, sl] = (_rope(head_pair(0, hd), cb, sb, B_QK_DIM // 4) * qb_scale).astype(BF16)
        kb_ref[0, :, sl] = _rope(head_pair(B_QK, hd), cb, sb, B_QK_DIM // 4).astype(BF16)
    c0 += 2 * B_QK
    pvu = jnp.dot(h, w_ref[:, c0:c0 + B_V + C_WIDTH], preferred_element_type=F32)
    for hd in range(B_HEADS):
        store_vt(vbt_ref, hd, pvu[:, hd * HEAD_DIM:(hd + 1) * HEAD_DIM])
    uc_ref[0] = pvu[:, B_V:B_V + C_WIDTH]


def _in_projection(xa, mod3, g1, w_in_p, gq, gk, rope):
    b, t, d = xa.shape
    nt = t // TM
    n_in = w_in_p.shape[1]
    tok = lambda w: pl.BlockSpec((1, TM, w), lambda bb, i: (bb, i, 0))
    tok_t = lambda w: pl.BlockSpec((1, w, TM), lambda bb, i: (bb, 0, i))
    const2 = lambda r, c: pl.BlockSpec((r, c), lambda bb, i: (0, 0))
    tab = pl.BlockSpec((TM, LANES), lambda bb, i: (i, 0))
    return pl.pallas_call(
        _proj_kernel,
        grid=(b, nt),
        in_specs=[
            tok(d),
            pl.BlockSpec((1, 1, mod3.shape[-1]), lambda bb, i: (jnp.where(i == 0, b, bb), 0, 0)),
            const2(1, d), const2(d, n_in), const2(1, HEAD_DIM), const2(1, HEAD_DIM),
            tab, tab, tab, tab,
        ],
        out_specs=[tok(A_Q), tok(A_KV), tok_t(A_KV_HEADS * VT_ROWS), tok(B_QK), tok(B_QK),
                   tok_t(B_HEADS * VT_ROWS), tok(C_WIDTH)],
        out_shape=[
            jax.ShapeDtypeStruct((b, t, A_Q), BF16),
            jax.ShapeDtypeStruct((b, t, A_KV), BF16),
            jax.ShapeDtypeStruct((b, A_KV_HEADS * VT_ROWS, t), BF16),
            jax.ShapeDtypeStruct((b, t, B_QK), BF16),
            jax.ShapeDtypeStruct((b, t, B_QK), BF16),
            jax.ShapeDtypeStruct((b, B_HEADS * VT_ROWS, t), BF16),
            jax.ShapeDtypeStruct((b, t, C_WIDTH), F32),
        ],
        compiler_params=_cparams(("parallel", "parallel"), VMEM_LIMIT),
        name="in_projection",
    )(xa, mod3, g1, w_in_p, gq, gk, *rope)


def _flash_t(q_stacks, kv_of, k_ref, vt_ref, m_ref, acc_ref, n_kv):
    def step(off, size):
        ss = [lax.dot_general(k_ref[0, pl.ds(off, size), kv * LANES:(kv + 1) * LANES], q,
                              (((1,), (1,)), ((), ())), preferred_element_type=F32)
              for kv, q in zip(kv_of, q_stacks)]
        for h, (kv, s) in enumerate(zip(kv_of, ss)):
            m = m_ref[h]
            m_new = jnp.maximum(m, jnp.max(s, axis=0, keepdims=True))
            p = jnp.exp2(s - m_new)
            vt = vt_ref[0, kv * VT_ROWS:(kv + 1) * VT_ROWS, pl.ds(off, size)]
            acc_ref[h] = (jnp.exp2(m - m_new) * acc_ref[h]
                          + jnp.dot(vt, p.astype(BF16), preferred_element_type=F32))
            m_ref[h] = m_new

    m_ref[...] = jnp.full(m_ref.shape, NEG_BIG, F32)
    acc_ref[...] = jnp.zeros(acc_ref.shape, F32)
    step(0, TM)

    def body(j, carry):
        step(pl.multiple_of(TM + j * KV_TILE, KV_TILE // 2), KV_TILE)
        return carry

    lax.fori_loop(0, n_kv, body, 0)
    return [acc_ref[h, 0:HEAD_DIM] / acc_ref[h, HEAD_DIM:HEAD_DIM + 1] for h in range(len(q_stacks))]


def _attn_a_kernel(q_ref, k_ref, vt_ref, o_ref, m_ref, acc_ref, *, q0, n_kv_full):
    qi = pl.program_id(1) + q0
    n_kv = jnp.where(qi == 0, 0, n_kv_full)
    q = q_ref[0]
    n_streams = A_HEADS // A_STREAM_HEADS
    q_stacks = [jnp.concatenate([q[:, (s * A_STREAM_HEADS + g) * HEAD_DIM:(s * A_STREAM_HEADS + g + 1) * HEAD_DIM]
                                 for g in range(A_STREAM_HEADS)], axis=0) for s in range(n_streams)]
    kv_of = [s * A_STREAM_HEADS // GQA_GROUP for s in range(n_streams)]
    for s, o in enumerate(_flash_t(q_stacks, kv_of, k_ref, vt_ref, m_ref, acc_ref, n_kv)):
        for g in range(A_STREAM_HEADS):
            c0 = (s * A_STREAM_HEADS + g) * HEAD_DIM
            o_ref[0, :, c0:c0 + HEAD_DIM] = o[:, g * TM:(g + 1) * TM].T.astype(BF16)


def _attn_b_kernel(q_ref, k_ref, vt_ref, lam_ref, gs_ref, o_ref, m_ref, acc_ref, *, q0, n_kv_full, lam_init):
    qi = pl.program_id(1) + q0
    n_kv = jnp.where(qi == 0, 0, n_kv_full)
    lane = lax.broadcasted_iota(I32, (TM, LANES), 1)
    zero = jnp.zeros((TM, LANES), BF16)
    q_stacks = []
    for h in range(B_HEADS):
        q = q_ref[0, :, h * LANES:(h + 1) * LANES]
        q_stacks.append(jnp.concatenate(
            [jnp.where(lane < B_QK_DIM, q, zero), jnp.where(lane >= B_QK_DIM, q, zero)], axis=0))
    res = _flash_t(q_stacks, list(range(B_HEADS)), k_ref, vt_ref, m_ref, acc_ref, n_kv)
    lv = lam_ref[...]
    lam = (jnp.exp(jnp.sum(lv[0:1] * lv[1:2], axis=-1, keepdims=True))
           - jnp.exp(jnp.sum(lv[2:3] * lv[3:4], axis=-1, keepdims=True)) + lam_init)
    for h, o in enumerate(res):
        od = o[:, 0:TM] - lam * o[:, TM:2 * TM]
        r = lax.rsqrt(jnp.mean(od * od, axis=0, keepdims=True) + EPS)
        y = od * r * gs_ref[...] * (1.0 - lam_init)
        o_ref[0, :, h * LANES:(h + 1) * LANES] = y.T.astype(BF16)


def _attention_a(qa, ka, vat, q0):
    b, t, _ = qa.shape
    nt = t // TM
    return pl.pallas_call(
        functools.partial(_attn_a_kernel, q0=q0, n_kv_full=(t - TM) // KV_TILE),
        grid=(b, nt - q0),
        in_specs=[
            pl.BlockSpec((1, TM, A_Q), lambda bb, i: (bb, i + q0, 0)),
            pl.BlockSpec((1, t, A_KV), lambda bb, i: (bb, 0, 0)),
            pl.BlockSpec((1, A_KV_HEADS * VT_ROWS, t), lambda bb, i: (bb, 0, 0)),
        ],
        out_specs=pl.BlockSpec((1, TM, A_Q), lambda bb, i: (bb, i, 0)),
        out_shape=jax.ShapeDtypeStruct((b, t - q0 * TM, A_Q), BF16),
        scratch_shapes=[pltpu.VMEM((A_HEADS // A_STREAM_HEADS, 1, A_STREAM_HEADS * TM), F32),
                        pltpu.VMEM((A_HEADS // A_STREAM_HEADS, VT_ROWS, A_STREAM_HEADS * TM), F32)],
        compiler_params=_cparams(("parallel", "parallel"), VMEM_LIMIT),
        name="attention_gqa",
    )(qa, ka, vat)


def _attention_b(qb, kb, vbt, lam_vecs, g_sub_col, q0, lam_init):
    b, t, _ = qb.shape
    nt = t // TM
    return pl.pallas_call(
        functools.partial(_attn_b_kernel, q0=q0, n_kv_full=(t - TM) // KV_TILE, lam_init=lam_init),
        grid=(b, nt - q0),
        in_specs=[
            pl.BlockSpec((1, TM, B_QK), lambda bb, i: (bb, i + q0, 0)),
            pl.BlockSpec((1, t, B_QK), lambda bb, i: (bb, 0, 0)),
            pl.BlockSpec((1, B_HEADS * VT_ROWS, t), lambda bb, i: (bb, 0, 0)),
            pl.BlockSpec((4, B_QK_DIM), lambda bb, i: (0, 0)),
            pl.BlockSpec((LANES, 1), lambda bb, i: (0, 0)),
        ],
        out_specs=pl.BlockSpec((1, TM, B_V), lambda bb, i: (bb, i, 0)),
        out_shape=jax.ShapeDtypeStruct((b, t - q0 * TM, B_V), BF16),
        scratch_shapes=[pltpu.VMEM((B_HEADS, 1, 2 * TM), F32), pltpu.VMEM((B_HEADS, VT_ROWS, 2 * TM), F32)],
        compiler_params=_cparams(("parallel", "parallel"), VMEM_LIMIT),
        name="attention_diff",
    )(qb, kb, vbt, lam_vecs, g_sub_col)


def _pool_kernel(prev_ref, cur_ref, next_ref, wp_ref, ps_ref, o_ref, *, q0, ctx_len, total_len):
    i = pl.program_id(1) + q0
    ext = jnp.concatenate([prev_ref[0], cur_ref[0], next_ref[0]], axis=0)
    rows = ext.shape[0]
    seg_lo = jnp.where(i == 0, 0, ctx_len)
    seg_hi = jnp.where(i == 0, ctx_len, total_len)
    grow = i * TM - POOL_HALO + lax.broadcasted_iota(I32, (rows, 1), 0)
    ext = jnp.where((grow >= seg_lo) & (grow < seg_hi), ext, 0.0)
    tpos = grow[POOL_HALO:POOL_HALO + TM] - seg_lo
    seg_len = seg_hi - seg_lo

    def back(a, d):
        return pltpu.roll(a, d, 0)

    def fwd(a, d):
        return pltpu.roll(a, rows - d, 0)

    for gi, w in enumerate(POOL_WINDOWS):
        u = ext[:, gi * POOL_GROUP:(gi + 1) * POOL_GROUP]
        acc = u + back(u, 1)
        span = 2
        while span < w:
            acc = acc + back(acc, span)
            span *= 2
        if w > 2:
            acc = fwd(acc, w // 2 - 1)
        cnt = (jnp.minimum(tpos + w // 2, seg_len) - jnp.maximum(tpos - w // 2, 0)).astype(F32)
        sl = slice(POOL_HALO, POOL_HALO + TM)
        p = (acc[sl] / cnt - u[sl]).astype(BF16)
        y = jnp.dot(p, wp_ref[gi], preferred_element_type=F32)
        o_ref[0, :, gi * POOL_GROUP:(gi + 1) * POOL_GROUP] = (
            y * ps_ref[:, gi * POOL_GROUP:(gi + 1) * POOL_GROUP]).astype(BF16)


def _pooling(uc, w_pool_b, pool_scale, q0, ctx_len):
    b, t, cw = uc.shape
    nt = t // TM
    per = TM // POOL_HALO
    last = t // POOL_HALO - 1
    return pl.pallas_call(
        functools.partial(_pool_kernel, q0=q0, ctx_len=ctx_len, total_len=t),
        grid=(b, nt - q0),
        in_specs=[
            pl.BlockSpec((1, POOL_HALO, cw), lambda bb, i: (bb, jnp.maximum((i + q0) * per - 1, 0), 0)),
            pl.BlockSpec((1, TM, cw), lambda bb, i: (bb, i + q0, 0)),
            pl.BlockSpec((1, POOL_HALO, cw), lambda bb, i: (bb, jnp.minimum((i + q0 + 1) * per, last), 0)),
            pl.BlockSpec(w_pool_b.shape, lambda bb, i: (0, 0, 0)),
            pl.BlockSpec((1, cw), lambda bb, i: (0, 0)),
        ],
        out_specs=pl.BlockSpec((1, TM, cw), lambda bb, i: (bb, i, 0)),
        out_shape=jax.ShapeDtypeStruct((b, t - q0 * TM, cw), BF16),
        compiler_params=_cparams(("parallel", "parallel")),
        name="pooling",
    )(uc, uc, uc, w_pool_b, pool_scale)


def _post_kernel(oa_ref, ob_ref, oc_ref, x_ref, mod_ref, g2_ref, wo_ref, wr_ref, br_ref,
                 xmid_ref, f_ref, te_ref, rk_ref, gt_ref, cnt_ref, carry_ref):
    d = x_ref.shape[-1]
    first = (pl.program_id(0) == 0) & (pl.program_id(1) == 0)

    @pl.when(first)
    def _():
        carry_ref[...] = jnp.zeros_like(carry_ref)

    m = (jnp.dot(oa_ref[0], wo_ref[0:A_Q], preferred_element_type=F32)
         + jnp.dot(ob_ref[0], wo_ref[A_Q:A_Q + B_V], preferred_element_type=F32)
         + jnp.dot(oc_ref[0], wo_ref[A_Q + B_V:A_Q + B_V + C_WIDTH], preferred_element_type=F32))
    x = x_ref[0] + mod_ref[0, :, 2 * d:3 * d] * m
    xmid_ref[0] = x
    xn = x * lax.rsqrt(jnp.mean(x * x, axis=-1, keepdims=True) + EPS) * g2_ref[...]
    f = xn * (1.0 + mod_ref[0, :, 4 * d:5 * d]) + mod_ref[0, :, 3 * d:4 * d]
    f_ref[0] = f

    f_hi = f.astype(BF16)
    f_lo = (f - f_hi.astype(F32)).astype(BF16)
    lg2 = jnp.dot(f_hi, wr_ref[...], preferred_element_type=F32)
    lg1 = jnp.dot(f_lo, wr_ref[:, 0:LANES], preferred_element_type=F32)
    logits = lg2[:, 0:LANES] + lg2[:, LANES:2 * LANES] + lg1 + br_ref[...]
    work = logits.T[0:N_EXPERTS]

    e_id = lax.broadcasted_iota(I32, work.shape, 0).astype(F32)
    vals, idxs, hots = [], [], []
    for _ in range(TOP_K):
        mx = jnp.max(work, axis=0, keepdims=True)
        idx = jnp.min(jnp.where(work == mx, e_id, float(N_EXPERTS)), axis=0, keepdims=True)
        hot = e_id == idx
        vals.append(mx)
        idxs.append(idx)
        hots.append(hot)
        work = jnp.where(hot, -jnp.inf, work)
    ex = [jnp.exp(v - vals[0]) for v in vals]
    den = ex[0] + ex[1] + ex[2] + ex[3]
    gates = [e / den for e in ex]

    msel = jnp.zeros(work.shape, F32)
    for hot in hots:
        msel = msel + jnp.where(hot, 1.0, 0.0)
    r_i = lax.broadcasted_iota(I32, (TM, TM), 0)
    c_i = lax.broadcasted_iota(I32, (TM, TM), 1)
    tri = jnp.where(r_i < c_i, 1.0, 0.0).astype(BF16)
    carry = carry_ref[:, 0:1]
    rank_full = jnp.dot(msel.astype(BF16), tri, preferred_element_type=F32) + carry
    ranks = [jnp.sum(jnp.where(hot, rank_full, 0.0), axis=0, keepdims=True) for hot in hots]
    new_carry = carry + jnp.sum(msel, axis=1, keepdims=True)
    carry_ref[...] = jnp.broadcast_to(new_carry, carry_ref.shape)
    cnt_ref[...] = jnp.broadcast_to(new_carry, cnt_ref.shape)

    row8 = lax.broadcasted_iota(I32, (8, TM), 0)

    def rows8(vs):
        out = jnp.zeros((8, TM), F32)
        for k, v in enumerate(vs):
            out = jnp.where(row8 == k, v, out)
        return out

    te_ref[...] = rows8(idxs).astype(I32)
    rk_ref[...] = rows8(ranks).astype(I32)
    g128 = jnp.concatenate([rows8(gates), jnp.zeros((LANES - 8, TM), F32)], axis=0)
    gt_ref[...] = g128.T


def _post_attention(oa, ob, oc, xa, mod3, g2, w_out_b, wr2, br, q0):
    b, t, d = xa.shape
    nt = t // TM - q0
    ntok = b * nt * TM
    tok = lambda w: pl.BlockSpec((1, TM, w), lambda bb, i: (bb, i, 0))
    const2 = lambda r, c: pl.BlockSpec((r, c), lambda bb, i: (0, 0))
    flat = lambda r: pl.BlockSpec((r, TM), lambda bb, i: (0, bb * nt + i))
    return pl.pallas_call(
        _post_kernel,
        grid=(b, nt),
        in_specs=[
            tok(A_Q), tok(B_V), tok(C_WIDTH),
            pl.BlockSpec((1, TM, d), lambda bb, i: (bb, i + q0, 0)),
            pl.BlockSpec((1, 1, mod3.shape[-1]), lambda bb, i: (jnp.where(i + q0 == 0, b, bb), 0, 0)),
            const2(1, d), const2(d, d), const2(d, 2 * LANES), const2(1, LANES),
        ],
        out_specs=[
            tok(d), tok(d), flat(8), flat(8),
            pl.BlockSpec((TM, LANES), lambda bb, i: (bb * nt + i, 0)),
            const2(N_EXPERTS, LANES),
        ],
        out_shape=[
            jax.ShapeDtypeStruct((b, nt * TM, d), F32),
            jax.ShapeDtypeStruct((b, nt * TM, d), F32),
            jax.ShapeDtypeStruct((8, ntok), I32),
            jax.ShapeDtypeStruct((8, ntok), I32),
            jax.ShapeDtypeStruct((ntok, LANES), F32),
            jax.ShapeDtypeStruct((N_EXPERTS, LANES), F32),
        ],
        scratch_shapes=[pltpu.VMEM((N_EXPERTS, LANES), F32)],
        compiler_params=_cparams(("arbitrary", "arbitrary"), VMEM_LIMIT),
        name="out_projection_router",
    )(oa, ob, oc, xa, mod3, g2, w_out_b, wr2, br)


def _row_copies(pos_ref, src_row, dst_row, sem):
    def issue(t, carry):
        for k in range(TOP_K):
            p = pos_ref[k * TM + t]
            pltpu.make_async_copy(src_row(k, t, p), dst_row(k, t, p), sem).start()
        return carry

    lax.fori_loop(0, TM, issue, 0, unroll=8)


def _dispatch_kernel(pos_ref, tend_ref, f_ref, xs_ref, zbuf, sem, zsem):
    first = (pl.program_id(0) == 0) & (pl.program_id(1) == 0)

    @pl.when(first)
    def _():
        zbuf[...] = jnp.zeros_like(zbuf)

        def last_tile_copy(e):
            return pltpu.make_async_copy(zbuf, xs_ref.at[pl.ds((tend_ref[e] - 1) * TM, TM)], zsem)

        def has_tiles(e):
            return tend_ref[e] > (tend_ref[e - 1] if e else 0)

        def tail_copy(i):
            return pltpu.make_async_copy(zbuf, xs_ref.at[pl.ds(i * TM, TM)], zsem)

        n_used, n_tiles = tend_ref[N_EXPERTS - 1], xs_ref.shape[0] // TM
        for e in range(N_EXPERTS):
            pl.when(has_tiles(e))(lambda e=e: last_tile_copy(e).start())
        lax.fori_loop(n_used, n_tiles, lambda i, c: (tail_copy(i).start(), c)[1], 0)
        for e in range(N_EXPERTS):
            pl.when(has_tiles(e))(lambda e=e: last_tile_copy(e).wait())
        lax.fori_loop(n_used, n_tiles, lambda i, c: (tail_copy(i).wait(), c)[1], 0)

    _row_copies(pos_ref,
                lambda k, t, p: f_ref.at[0, pl.ds(t, 1)],
                lambda k, t, p: xs_ref.at[pl.ds(p, 1)], sem)
    for k in range(TOP_K):
        pltpu.make_async_copy(f_ref.at[0], xs_ref.at[pl.ds(0, TM)], sem).wait()


def _dispatch(pos_flat, tile_end, f, n_rows):
    b, t, hw = f.shape
    nt = t // TM
    return pl.pallas_call(
        _dispatch_kernel,
        grid=(b, nt),
        in_specs=[
            pl.BlockSpec((TOP_K * TM,), lambda bb, i: (bb * nt + i,), memory_space=pltpu.SMEM),
            pl.BlockSpec(memory_space=pltpu.SMEM),
            pl.BlockSpec((1, TM, hw), lambda bb, i: (bb, i, 0)),
        ],
        out_specs=pl.BlockSpec(memory_space=pl.ANY),
        out_shape=jax.ShapeDtypeStruct((n_rows, hw), F32),
        scratch_shapes=[pltpu.VMEM((TM, hw), F32), pltpu.SemaphoreType.DMA(()), pltpu.SemaphoreType.DMA(())],
        compiler_params=_cparams(("arbitrary", "arbitrary")),
        name="moe_dispatch",
    )(pos_flat, tile_end, f)


def _deinterleave_kernel(w_ref, p_ref, o_ref):
    fdim = w_ref.shape[2] // 2
    for j in range(fdim // LANES):
        chunk = w_ref[0, :, 2 * LANES * j:2 * LANES * (j + 1)].astype(BF16)
        r = jnp.dot(chunk, p_ref[...], preferred_element_type=F32)
        o_ref[0, :, LANES * j:LANES * (j + 1)] = r[:, 0:LANES].astype(BF16)
        o_ref[0, :, fdim + LANES * j:fdim + LANES * (j + 1)] = r[:, LANES:2 * LANES].astype(BF16)


def _deinterleave_weights(w, perm):
    n, d, f2 = w.shape
    tk = 512
    return pl.pallas_call(
        _deinterleave_kernel,
        grid=(n, d // tk),
        in_specs=[
            pl.BlockSpec((1, tk, f2), lambda e, j: (e, j, 0)),
            pl.BlockSpec(perm.shape, lambda e, j: (0, 0)),
        ],
        out_specs=pl.BlockSpec((1, tk, f2), lambda e, j: (e, j, 0)),
        out_shape=jax.ShapeDtypeStruct((n, d, f2), BF16),
        compiler_params=_cparams(("parallel", "parallel")),
        name="expert_weight_layout",
    )(w, perm)


def _experts_kernel(te_ref, tb_ref, nu_ref, x_ref, w1_ref, b1_ref, w2_ref, b2_ref, y_ref, w2b_ref):
    del tb_ref
    fdim = w2_ref.shape[1]
    i = pl.program_id(0)

    @pl.when(i >= nu_ref[0])
    def _():
        y_ref[...] = jnp.zeros_like(y_ref)

    @pl.when((i < nu_ref[0]) & ((i == 0) | (te_ref[i] != te_ref[jnp.maximum(i - 1, 0)])))
    def _():
        w2b_ref[...] = w2_ref[0].astype(BF16)

    @pl.when(i < nu_ref[0])
    def _():
        u = jnp.dot(x_ref[...].astype(BF16), w1_ref[0], preferred_element_type=F32) + b1_ref[0]
        glu = jnp.minimum(u[:, 0:fdim], SWIGLU_LIMIT)
        lin = jnp.clip(u[:, fdim:2 * fdim], -SWIGLU_LIMIT, SWIGLU_LIMIT)
        a = glu * _sigmoid(SWIGLU_ALPHA * glu) * (lin + 1.0)
        y_ref[...] = jnp.dot(a.astype(BF16), w2b_ref[...], preferred_element_type=F32) + b2_ref[0]


def _experts(tile_expert, tile_block, n_used, xs, w1p, b1p, w2, b2):
    n_rows, hw = xs.shape
    n_tiles = n_rows // TM
    _, d, f2 = w1p.shape
    fdim = f2 // 2
    grid_spec = pltpu.PrefetchScalarGridSpec(
        num_scalar_prefetch=3,
        grid=(n_tiles,),
        in_specs=[
            pl.BlockSpec((TM, hw), lambda i, te, tb, nu: (tb[i], 0)),
            pl.BlockSpec((1, d, f2), lambda i, te, tb, nu: (te[i], 0, 0)),
            pl.BlockSpec((1, 1, f2), lambda i, te, tb, nu: (te[i], 0, 0)),
            pl.BlockSpec((1, fdim, d), lambda i, te, tb, nu: (te[i], 0, 0)),
            pl.BlockSpec((1, 1, d), lambda i, te, tb, nu: (te[i], 0, 0)),
        ],
        out_specs=pl.BlockSpec((TM, hw), lambda i, te, tb, nu: (i, 0)),
        scratch_shapes=[pltpu.VMEM((fdim, d), BF16)],
    )
    return pl.pallas_call(
        _experts_kernel,
        grid_spec=grid_spec,
        out_shape=jax.ShapeDtypeStruct((n_rows, hw), F32),
        compiler_params=_cparams(("arbitrary",), VMEM_LIMIT),
        name="moe_experts",
    )(tile_expert, tile_block, n_used, xs, w1p, b1p, w2, b2)


def _combine_kernel(pos_ref, ys_ref, gt_ref, x_ref, mod_ref, gf_ref, o_ref, ybuf, sem, *, final):
    d = x_ref.shape[-1]
    _row_copies(pos_ref,
                lambda k, t, p: ys_ref.at[pl.ds(p, 1)],
                lambda k, t, p: ybuf.at[k, pl.ds(t, 1)], sem)
    for k in range(TOP_K):
        pltpu.make_async_copy(ys_ref.at[pl.ds(0, TM)], ybuf.at[k], sem).wait()
    acc = gt_ref[:, 0:1] * ybuf[0]
    for k in range(1, TOP_K):
        acc = acc + gt_ref[:, k:k + 1] * ybuf[k]
    xo = x_ref[0] + mod_ref[0, :, 5 * d:6 * d] * acc
    if final:
        xo = xo * lax.rsqrt(jnp.mean(xo * xo, axis=-1, keepdims=True) + EPS) * gf_ref[...]
    o_ref[0] = xo


def _combine(pos_flat, ys, gate_t, xmid, mod3, g_final, q0, final):
    b, t, d = xmid.shape
    nt = t // TM
    return pl.pallas_call(
        functools.partial(_combine_kernel, final=final),
        grid=(b, nt),
        in_specs=[
            pl.BlockSpec((TOP_K * TM,), lambda bb, i: (bb * nt + i,), memory_space=pltpu.SMEM),
            pl.BlockSpec(memory_space=pl.ANY),
            pl.BlockSpec((TM, LANES), lambda bb, i: (bb * nt + i, 0)),
            pl.BlockSpec((1, TM, d), lambda bb, i: (bb, i, 0)),
            pl.BlockSpec((1, 1, mod3.shape[-1]), lambda bb, i: (jnp.where(i + q0 == 0, b, bb), 0, 0)),
            pl.BlockSpec((1, d), lambda bb, i: (0, 0)),
        ],
        out_specs=pl.BlockSpec((1, TM, d), lambda bb, i: (bb, i, 0)),
        out_shape=jax.ShapeDtypeStruct((b, t, d), F32),
        scratch_shapes=[pltpu.VMEM((TOP_K, TM, d), F32), pltpu.SemaphoreType.DMA(())],
        compiler_params=_cparams(("arbitrary", "arbitrary"), VMEM_LIMIT),
        name="moe_combine",
    )(pos_flat, ys, gate_t, xmid, mod3, g_final)


def _routing_tables(top_e, rank, counts, n_tiles):
    cnt = counts[:, 0].astype(I32)
    tiles_e = (cnt + TM - 1) // TM
    tile_end = jnp.cumsum(tiles_e)
    row_start = (tile_end - tiles_e) * TM
    hit = top_e[0:TOP_K, :, None] == jnp.arange(N_EXPERTS, dtype=I32)
    pos = jnp.sum(jnp.where(hit, row_start, 0), axis=-1) + rank[0:TOP_K]
    ntok = pos.shape[1]
    pos_flat = pos.reshape(TOP_K, ntok // TM, TM).transpose(1, 0, 2).reshape(-1)
    n_used = tile_end[-1]
    ti = jnp.minimum(jnp.arange(n_tiles, dtype=I32), n_used - 1)
    tile_expert = jnp.minimum(jnp.sum((tile_end[None, :] <= ti[:, None]).astype(I32), axis=1), N_EXPERTS - 1)
    return pos_flat, tile_expert, ti, n_used.reshape(1).astype(I32), tile_end.astype(I32)


def _rope_tables(ctx_len, seq, head_dim, lanes_per_group):
    t = np.arange(seq)
    row, col = t // GRID_W, t % GRID_W
    quarter = head_dim // 4
    inv = ROPE_THETA ** (-np.arange(0, head_dim // 2, 2, dtype=np.float64) / (head_dim // 2))
    lane = np.arange(LANES) % lanes_per_group
    is_col = (lane // (head_dim // 2)) % 2 == 1
    w = lane % (head_dim // 2)
    freq = inv[w % quarter]
    pos = np.where(is_col[None, :], col[:, None], row[:, None]).astype(np.float64)
    ang = pos * freq[None, :]
    sign = np.where(w < quarter, -1.0, 1.0)[None, :]
    cos = np.concatenate([np.ones((ctx_len, LANES)), np.cos(ang)], axis=0)
    sin = np.concatenate([np.zeros((ctx_len, LANES)), np.sin(ang) * sign], axis=0)
    return jnp.asarray(cos, F32), jnp.asarray(sin, F32)


def _deinterleave_matrix():
    p = np.zeros((2 * LANES, 2 * LANES), np.float32)
    i = np.arange(LANES)
    p[2 * i, i] = 1.0
    p[2 * i + 1, LANES + i] = 1.0
    return jnp.asarray(p, BF16)


def kernel(x, c, ctx, c_ctx, w_mod, b_mod, g_norm1, g_norm2, w_in, g_qnorm, g_knorm, lambda_q1, lambda_k1,
           lambda_q2, lambda_k2, g_subln, w_pool, pool_scale, w_out, w_router, b_router, w_expert_in,
           b_expert_in, w_expert_out, b_expert_out, g_final):
    b, seq, d = x.shape
    ctx_len = ctx.shape[1]
    depth = w_mod.shape[0]
    assert ctx_len == TM and seq % TM == 0

    xa = jnp.concatenate([ctx, x], axis=1)
    cc = jnp.concatenate([c, c_ctx[None, :], jnp.zeros((8 - b - 1, d), F32)], axis=0)
    mods = _modulation(cc, w_mod, b_mod)
    rope_a = _rope_tables(ctx_len, seq, HEAD_DIM, HEAD_DIM)
    rope_b = _rope_tables(ctx_len, seq, B_QK_DIM, B_QK_DIM)
    e, _, f2 = w_expert_in.shape[1:]
    w1p_all = _deinterleave_weights(w_expert_in.reshape(depth * e, d, f2), _deinterleave_matrix())
    b1p_all = jnp.concatenate([b_expert_in[..., 0::2], b_expert_in[..., 1::2]], axis=-1).reshape(depth * e, 1, f2)
    w2_all = w_expert_out.reshape(depth * e, f2 // 2, d)
    b2_all = b_expert_out.reshape(depth * e, 1, d)

    for l in range(depth):
        last = l == depth - 1
        q0 = 1 if last else 0
        lam_init = 0.8 - 0.6 * math.exp(-0.3 * l)
        mod3 = mods[l].reshape(8, 1, 6 * d)
        qa, ka, vat, qb, kb, vbt, uc = _in_projection(
            xa, mod3, g_norm1[l][None], w_in[l].astype(BF16), g_qnorm[l][None], g_knorm[l][None],
            rope_a + rope_b)
        oa = _attention_a(qa, ka, vat, q0)
        lam_vecs = jnp.stack([lambda_q1[l], lambda_k1[l], lambda_q2[l], lambda_k2[l]])
        ob = _attention_b(qb, kb, vbt, lam_vecs, g_subln[l][:, None], q0, lam_init)
        oc = _pooling(uc, w_pool[l].astype(BF16), pool_scale[l][None], q0, ctx_len)

        wr = jnp.pad(w_router[l], ((0, 0), (0, LANES - e)))
        wr_hi = wr.astype(BF16)
        wr2 = jnp.concatenate([wr_hi, (wr - wr_hi.astype(F32)).astype(BF16)], axis=1)
        br = jnp.pad(b_router[l], (0, LANES - e))[None]
        xmid, f_moe, top_e, rank, gate_t, counts = _post_attention(
            oa, ob, oc, xa, mod3, g_norm2[l][None], w_out[l].astype(BF16), wr2, br, q0)

        ntok = top_e.shape[1]
        n_tiles = (TOP_K * ntok) // TM + e
        pos_flat, tile_expert, tile_block, n_used, tile_end = _routing_tables(top_e, rank, counts, n_tiles)
        xs = _dispatch(pos_flat, tile_end, f_moe, n_tiles * TM)
        ys = _experts(tile_expert + l * e, tile_block, n_used, xs, w1p_all, b1p_all, w2_all, b2_all)
        xa = _combine(pos_flat, ys, gate_t, xmid, mod3, g_final[None], q0, last)
    return xa
```

```python
import functools
import math

import numpy as np
import jax
import jax.numpy as jnp
from jax import lax
from jax.experimental import pallas as pl
from jax.experimental.pallas import tpu as pltpu

F32 = jnp.float32
BF16 = jnp.bfloat16
U32 = jnp.uint32
I32 = jnp.int32

HEAD_DIM = 128
A_HEADS = 8
A_KV_HEADS = 2
GQA_GROUP = A_HEADS // A_KV_HEADS
B_HEADS = 4
B_QK_DIM = 64
POOL_WINDOWS = (2, 4, 8, 16)
POOL_GROUP = 128
C_WIDTH = len(POOL_WINDOWS) * POOL_GROUP
A_Q = A_HEADS * HEAD_DIM
A_KV = A_KV_HEADS * HEAD_DIM
B_QK = B_HEADS * 2 * B_QK_DIM
B_V = B_HEADS * 2 * B_QK_DIM
N_EXPERTS = 32
TOP_K = 4
SWIGLU_LIMIT = 7.0
SWIGLU_ALPHA = 1.702
ROPE_THETA = 10000.0
GRID_W = 64
EPS = 1e-6

TM = 256
KV_TILE = 512
BF16_ROWS = 16
VT_ROWS = 128 + BF16_ROWS
A_STREAM_HEADS = 2
E_TM = 512
POOL_HALO = 8
LANES = 128
VMEM_LIMIT = 56 * 1024 * 1024
LOG2E = math.log2(math.e)
NEG_BIG = -1e30


def _cparams(sem, vmem=None):
    return pltpu.CompilerParams(dimension_semantics=sem, vmem_limit_bytes=vmem)


def _sigmoid(z):
    return 1.0 / (1.0 + jnp.exp(-z))


def _mod_kernel(cc_ref, w_ref, b_ref, o_ref):
    cc = cc_ref[...]
    a = (cc * _sigmoid(cc)).astype(BF16)
    o_ref[0] = jnp.dot(a, w_ref[0].astype(BF16), preferred_element_type=F32) + b_ref[0]


def _modulation(cc, w_mod, b_mod):
    depth, d, n = w_mod.shape
    tn = 1536
    return pl.pallas_call(
        _mod_kernel,
        grid=(depth, n // tn),
        in_specs=[
            pl.BlockSpec((8, d), lambda l, j: (0, 0)),
            pl.BlockSpec((1, d, tn), lambda l, j: (l, 0, j)),
            pl.BlockSpec((1, 1, tn), lambda l, j: (l, 0, j)),
        ],
        out_specs=pl.BlockSpec((1, 8, tn), lambda l, j: (l, 0, j)),
        out_shape=jax.ShapeDtypeStruct((depth, 8, n), F32),
        compiler_params=_cparams(("parallel", "parallel"), VMEM_LIMIT),
        name="modulation",
    )(cc, w_mod, b_mod.reshape(depth, 1, n))


def _rope(y, cos, sin_signed, half):
    width = y.shape[-1]
    lane = lax.broadcasted_iota(I32, y.shape, 1)
    partner = jnp.where(lane % (2 * half) < half,
                        pltpu.roll(y, width - half, 1), pltpu.roll(y, half, 1))
    return y * cos + partner * sin_signed


def _head_rms(y, g):
    return y * lax.rsqrt(jnp.mean(y * y, axis=-1, keepdims=True) + EPS) * g


def _proj_kernel(x_ref, mod_ref, g1_ref, w_ref, gq_ref, gk_ref, ca_ref, sa_ref, cb_ref, sb_ref,
                 qa_ref, ka_ref, vat_ref, qb_ref, kb_ref, vbt_ref, uc_ref):
    d = x_ref.shape[-1]
    x = x_ref[0]
    xn = x * lax.rsqrt(jnp.mean(x * x, axis=-1, keepdims=True) + EPS) * g1_ref[...]
    sh = mod_ref[0, :, 0:d]
    sc = mod_ref[0, :, d:2 * d]
    h = (xn * (1.0 + sc) + sh).astype(BF16)
    ca, sa, cb, sb = ca_ref[...], sa_ref[...], cb_ref[...], sb_ref[...]

    qa_scale = LOG2E / math.sqrt(HEAD_DIM)
    qb_scale = LOG2E / math.sqrt(B_QK_DIM)
    ones = jnp.ones((BF16_ROWS, x.shape[0]), BF16)

    def store_vt(ref, hd, v):
        ref[0, hd * VT_ROWS:hd * VT_ROWS + HEAD_DIM, :] = v.T.astype(BF16)
        ref[0, hd * VT_ROWS + HEAD_DIM:(hd + 1) * VT_ROWS, :] = ones

    c0 = 0
    pq = jnp.dot(h, w_ref[:, c0:c0 + A_Q], preferred_element_type=F32)
    for hd in range(A_HEADS):
        y = _head_rms(pq[:, hd * HEAD_DIM:(hd + 1) * HEAD_DIM], gq_ref[...])
        qa_ref[0, :, hd * HEAD_DIM:(hd + 1) * HEAD_DIM] = (
            _rope(y, ca, sa, HEAD_DIM // 4) * qa_scale).astype(BF16)
    c0 += A_Q
    pkv = jnp.dot(h, w_ref[:, c0:c0 + 2 * A_KV], preferred_element_type=F32)
    for hd in range(A_KV_HEADS):
        y = _head_rms(pkv[:, hd * HEAD_DIM:(hd + 1) * HEAD_DIM], gk_ref[...])
        ka_ref[0, :, hd * HEAD_DIM:(hd + 1) * HEAD_DIM] = _rope(y, ca, sa, HEAD_DIM // 4).astype(BF16)
        store_vt(vat_ref, hd, pkv[:, A_KV + hd * HEAD_DIM:A_KV + (hd + 1) * HEAD_DIM])
    c0 += 2 * A_KV
    pb = jnp.dot(h, w_ref[:, c0:c0 + 2 * B_QK], preferred_element_type=F32)
    lane = lax.broadcasted_iota(I32, (x.shape[0], LANES), 1)

    def head_pair(base, hd):
        v0 = pb[:, base + (hd // 2) * LANES:base + (hd // 2 + 1) * LANES]
        v1 = pb[:, base + B_QK // 2 + (hd // 2) * LANES:base + B_QK // 2 + (hd // 2 + 1) * LANES]
        if hd % 2 == 0:
            return jnp.where(lane < B_QK_DIM, v0, pltpu.roll(v1, B_QK_DIM, 1))
        return jnp.where(lane < B_QK_DIM, pltpu.roll(v0, B_QK_DIM, 1), v1)

    for hd in range(B_HEADS):
        sl = slice(hd * LANES, (hd + 1) * LANES)
        qb_ref[0, :, sl] = (_rope(head_pair(0, hd), cb, sb, B_QK_DIM // 4) * qb_scale).astype(BF16)
        kb_ref[0, :, sl] = _rope(head_pair(B_QK, hd), cb, sb, B_QK_DIM // 4).astype(BF16)
    c0 += 2 * B_QK
    pvu = jnp.dot(h, w_ref[:, c0:c0 + B_V + C_WIDTH], preferred_element_type=F32)
    for hd in range(B_HEADS):
        store_vt(vbt_ref, hd, pvu[:, hd * HEAD_DIM:(hd + 1) * HEAD_DIM])
    uc_ref[0] = pvu[:, B_V:B_V + C_WIDTH]


def _in_projection(xa, mod3, g1, w_in_p, gq, gk, rope):
    b, t, d = xa.shape
    nt = t // TM
    n_in = w_in_p.shape[1]
    tok = lambda w: pl.BlockSpec((1, TM, w), lambda bb, i: (bb, i, 0))
    tok_t = lambda w: pl.BlockSpec((1, w, TM), lambda bb, i: (bb, 0, i))
    const2 = lambda r, c: pl.BlockSpec((r, c), lambda bb, i: (0, 0))
    tab = pl.BlockSpec((TM, LANES), lambda bb, i: (i, 0))
    return pl.pallas_call(
        _proj_kernel,
        grid=(b, nt),
        in_specs=[
            tok(d),
            pl.BlockSpec((1, 1, mod3.shape[-1]), lambda bb, i: (jnp.where(i == 0, b, bb), 0, 0)),
            const2(1, d), const2(d, n_in), const2(1, HEAD_DIM), const2(1, HEAD_DIM),
            tab, tab, tab, tab,
        ],
        out_specs=[tok(A_Q), tok(A_KV), tok_t(A_KV_HEADS * VT_ROWS), tok(B_QK), tok(B_QK),
                   tok_t(B_HEADS * VT_ROWS), tok(C_WIDTH)],
        out_shape=[
            jax.ShapeDtypeStruct((b, t, A_Q), BF16),
            jax.ShapeDtypeStruct((b, t, A_KV), BF16),
            jax.ShapeDtypeStruct((b, A_KV_HEADS * VT_ROWS, t), BF16),
            jax.ShapeDtypeStruct((b, t, B_QK), BF16),
            jax.ShapeDtypeStruct((b, t, B_QK), BF16),
            jax.ShapeDtypeStruct((b, B_HEADS * VT_ROWS, t), BF16),
            jax.ShapeDtypeStruct((b, t, C_WIDTH), F32),
        ],
        compiler_params=_cparams(("parallel", "parallel"), VMEM_LIMIT),
        name="in_projection",
    )(xa, mod3, g1, w_in_p, gq, gk, *rope)


def _flash_t(q_stacks, kv_of, k_ref, vt_ref, m_ref, acc_ref, n_kv):
    def step(off, size):
        ss = [lax.dot_general(k_ref[0, pl.ds(off, size), kv * LANES:(kv + 1) * LANES], q,
                              (((1,), (1,)), ((), ())), preferred_element_type=F32)
              for kv, q in zip(kv_of, q_stacks)]
        for h, (kv, s) in enumerate(zip(kv_of, ss)):
            m = m_ref[h]
            m_new = jnp.maximum(m, jnp.max(s, axis=0, keepdims=True))
            p = jnp.exp2(s - m_new)
            vt = vt_ref[0, kv * VT_ROWS:(kv + 1) * VT_ROWS, pl.ds(off, size)]
            acc_ref[h] = (jnp.exp2(m - m_new) * acc_ref[h]
                          + jnp.dot(vt, p.astype(BF16), preferred_element_type=F32))
            m_ref[h] = m_new

    m_ref[...] = jnp.full(m_ref.shape, NEG_BIG, F32)
    acc_ref[...] = jnp.zeros(acc_ref.shape, F32)
    step(0, TM)

    def body(j, carry):
        step(pl.multiple_of(TM + j * KV_TILE, KV_TILE // 2), KV_TILE)
        return carry

    lax.fori_loop(0, n_kv, body, 0)
    return [acc_ref[h, 0:HEAD_DIM] / acc_ref[h, HEAD_DIM:HEAD_DIM + 1] for h in range(len(q_stacks))]


def _attn_a_kernel(q_ref, k_ref, vt_ref, o_ref, m_ref, acc_ref, *, q0, n_kv_full):
    qi = pl.program_id(1) + q0
    n_kv = jnp.where(qi == 0, 0, n_kv_full)
    q = q_ref[0]
    n_streams = A_HEADS // A_STREAM_HEADS
    q_stacks = [jnp.concatenate([q[:, (s * A_STREAM_HEADS + g) * HEAD_DIM:(s * A_STREAM_HEADS + g + 1) * HEAD_DIM]
                                 for g in range(A_STREAM_HEADS)], axis=0) for s in range(n_streams)]
    kv_of = [s * A_STREAM_HEADS // GQA_GROUP for s in range(n_streams)]
    for s, o in enumerate(_flash_t(q_stacks, kv_of, k_ref, vt_ref, m_ref, acc_ref, n_kv)):
        for g in range(A_STREAM_HEADS):
            c0 = (s * A_STREAM_HEADS + g) * HEAD_DIM
            o_ref[0, :, c0:c0 + HEAD_DIM] = o[:, g * TM:(g + 1) * TM].T.astype(BF16)


def _attn_b_kernel(q_ref, k_ref, vt_ref, lam_ref, gs_ref, o_ref, m_ref, acc_ref, *, q0, n_kv_full, lam_init):
    qi = pl.program_id(1) + q0
    n_kv = jnp.where(qi == 0, 0, n_kv_full)
    lane = lax.broadcasted_iota(I32, (TM, LANES), 1)
    zero = jnp.zeros((TM, LANES), BF16)
    q_stacks = []
    for h in range(B_HEADS):
        q = q_ref[0, :, h * LANES:(h + 1) * LANES]
        q_stacks.append(jnp.concatenate(
            [jnp.where(lane < B_QK_DIM, q, zero), jnp.where(lane >= B_QK_DIM, q, zero)], axis=0))
    res = _flash_t(q_stacks, list(range(B_HEADS)), k_ref, vt_ref, m_ref, acc_ref, n_kv)
    lv = lam_ref[...]
    lam = (jnp.exp(jnp.sum(lv[0:1] * lv[1:2], axis=-1, keepdims=True))
           - jnp.exp(jnp.sum(lv[2:3] * lv[3:4], axis=-1, keepdims=True)) + lam_init)
    for h, o in enumerate(res):
        od = o[:, 0:TM] - lam * o[:, TM:2 * TM]
        r = lax.rsqrt(jnp.mean(od * od, axis=0, keepdims=True) + EPS)
        y = od * r * gs_ref[...] * (1.0 - lam_init)
        o_ref[0, :, h * LANES:(h + 1) * LANES] = y.T.astype(BF16)


def _attention_a(qa, ka, vat, q0):
    b, t, _ = qa.shape
    nt = t // TM
    return pl.pallas_call(
        functools.partial(_attn_a_kernel, q0=q0, n_kv_full=(t - TM) // KV_TILE),
        grid=(b, nt - q0),
        in_specs=[
            pl.BlockSpec((1, TM, A_Q), lambda bb, i: (bb, i + q0, 0)),
            pl.BlockSpec((1, t, A_KV), lambda bb, i: (bb, 0, 0)),
            pl.BlockSpec((1, A_KV_HEADS * VT_ROWS, t), lambda bb, i: (bb, 0, 0)),
        ],
        out_specs=pl.BlockSpec((1, TM, A_Q), lambda bb, i: (bb, i, 0)),
        out_shape=jax.ShapeDtypeStruct((b, t - q0 * TM, A_Q), BF16),
        scratch_shapes=[pltpu.VMEM((A_HEADS // A_STREAM_HEADS, 1, A_STREAM_HEADS * TM), F32),
                        pltpu.VMEM((A_HEADS // A_STREAM_HEADS, VT_ROWS, A_STREAM_HEADS * TM), F32)],
        compiler_params=_cparams(("parallel", "parallel"), VMEM_LIMIT),
        name="attention_gqa",
    )(qa, ka, vat)


def _attention_b(qb, kb, vbt, lam_vecs, g_sub_col, q0, lam_init):
    b, t, _ = qb.shape
    nt = t // TM
    return pl.pallas_call(
        functools.partial(_attn_b_kernel, q0=q0, n_kv_full=(t - TM) // KV_TILE, lam_init=lam_init),
        grid=(b, nt - q0),
        in_specs=[
            pl.BlockSpec((1, TM, B_QK), lambda bb, i: (bb, i + q0, 0)),
            pl.BlockSpec((1, t, B_QK), lambda bb, i: (bb, 0, 0)),
            pl.BlockSpec((1, B_HEADS * VT_ROWS, t), lambda bb, i: (bb, 0, 0)),
            pl.BlockSpec((4, B_QK_DIM), lambda bb, i: (0, 0)),
            pl.BlockSpec((LANES, 1), lambda bb, i: (0, 0)),
        ],
        out_specs=pl.BlockSpec((1, TM, B_V), lambda bb, i: (bb, i, 0)),
        out_shape=jax.ShapeDtypeStruct((b, t - q0 * TM, B_V), BF16),
        scratch_shapes=[pltpu.VMEM((B_HEADS, 1, 2 * TM), F32), pltpu.VMEM((B_HEADS, VT_ROWS, 2 * TM), F32)],
        compiler_params=_cparams(("parallel", "parallel"), VMEM_LIMIT),
        name="attention_diff",
    )(qb, kb, vbt, lam_vecs, g_sub_col)


def _pool_kernel(prev_ref, cur_ref, next_ref, wp_ref, ps_ref, o_ref, *, q0, ctx_len, total_len):
    i = pl.program_id(1) + q0
    ext = jnp.concatenate([prev_ref[0], cur_ref[0], next_ref[0]], axis=0)
    rows = ext.shape[0]
    seg_lo = jnp.where(i == 0, 0, ctx_len)
    seg_hi = jnp.where(i == 0, ctx_len, total_len)
    grow = i * TM - POOL_HALO + lax.broadcasted_iota(I32, (rows, 1), 0)
    ext = jnp.where((grow >= seg_lo) & (grow < seg_hi), ext, 0.0)
    tpos = grow[POOL_HALO:POOL_HALO + TM] - seg_lo
    seg_len = seg_hi - seg_lo

    def back(a, d):
        return pltpu.roll(a, d, 0)

    def fwd(a, d):
        return pltpu.roll(a, rows - d, 0)

    for gi, w in enumerate(POOL_WINDOWS):
        u = ext[:, gi * POOL_GROUP:(gi + 1) * POOL_GROUP]
        acc = u + back(u, 1)
        span = 2
        while span < w:
            acc = acc + back(acc, span)
            span *= 2
        if w > 2:
            acc = fwd(acc, w // 2 - 1)
        cnt = (jnp.minimum(tpos + w // 2, seg_len) - jnp.maximum(tpos - w // 2, 0)).astype(F32)
        sl = slice(POOL_HALO, POOL_HALO + TM)
        p = (acc[sl] / cnt - u[sl]).astype(BF16)
        y = jnp.dot(p, wp_ref[gi], preferred_element_type=F32)
        o_ref[0, :, gi * POOL_GROUP:(gi + 1) * POOL_GROUP] = (
            y * ps_ref[:, gi * POOL_GROUP:(gi + 1) * POOL_GROUP]).astype(BF16)


def _pooling(uc, w_pool_b, pool_scale, q0, ctx_len):
    b, t, cw = uc.shape
    nt = t // TM
    per = TM // POOL_HALO
    last = t // POOL_HALO - 1
    return pl.pallas_call(
        functools.partial(_pool_kernel, q0=q0, ctx_len=ctx_len, total_len=t),
        grid=(b, nt - q0),
        in_specs=[
            pl.BlockSpec((1, POOL_HALO, cw), lambda bb, i: (bb, jnp.maximum((i + q0) * per - 1, 0), 0)),
            pl.BlockSpec((1, TM, cw), lambda bb, i: (bb, i + q0, 0)),
            pl.BlockSpec((1, POOL_HALO, cw), lambda bb, i: (bb, jnp.minimum((i + q0 + 1) * per, last), 0)),
            pl.BlockSpec(w_pool_b.shape, lambda bb, i: (0, 0, 0)),
            pl.BlockSpec((1, cw), lambda bb, i: (0, 0)),
        ],
        out_specs=pl.BlockSpec((1, TM, cw), lambda bb, i: (bb, i, 0)),
        out_shape=jax.ShapeDtypeStruct((b, t - q0 * TM, cw), BF16),
        compiler_params=_cparams(("parallel", "parallel")),
        name="pooling",
    )(uc, uc, uc, w_pool_b, pool_scale)


def _post_kernel(oa_ref, ob_ref, oc_ref, x_ref, mod_ref, g2_ref, wo_ref, wr_ref, br_ref,
                 xmid_ref, f_ref, te_ref, rk_ref, gt_ref, cnt_ref, carry_ref):
    d = x_ref.shape[-1]
    first = (pl.program_id(0) == 0) & (pl.program_id(1) == 0)

    @pl.when(first)
    def _():
        carry_ref[...] = jnp.zeros_like(carry_ref)

    m = (jnp.dot(oa_ref[0], wo_ref[0:A_Q], preferred_element_type=F32)
         + jnp.dot(ob_ref[0], wo_ref[A_Q:A_Q + B_V], preferred_element_type=F32)
         + jnp.dot(oc_ref[0], wo_ref[A_Q + B_V:A_Q + B_V + C_WIDTH], preferred_element_type=F32))
    x = x_ref[0] + mod_ref[0, :, 2 * d:3 * d] * m
    xmid_ref[0] = x
    xn = x * lax.rsqrt(jnp.mean(x * x, axis=-1, keepdims=True) + EPS) * g2_ref[...]
    f = xn * (1.0 + mod_ref[0, :, 4 * d:5 * d]) + mod_ref[0, :, 3 * d:4 * d]
    f_ref[0] = f

    f_hi = f.astype(BF16)
    f_lo = (f - f_hi.astype(F32)).astype(BF16)
    lg2 = jnp.dot(f_hi, wr_ref[...], preferred_element_type=F32)
    lg1 = jnp.dot(f_lo, wr_ref[:, 0:LANES], preferred_element_type=F32)
    logits = lg2[:, 0:LANES] + lg2[:, LANES:2 * LANES] + lg1 + br_ref[...]
    work = logits.T[0:N_EXPERTS]

    e_id = lax.broadcasted_iota(I32, work.shape, 0).astype(F32)
    vals, idxs, hots = [], [], []
    for _ in range(TOP_K):
        mx = jnp.max(work, axis=0, keepdims=True)
        idx = jnp.min(jnp.where(work == mx, e_id, float(N_EXPERTS)), axis=0, keepdims=True)
        hot = e_id == idx
        vals.append(mx)
        idxs.append(idx)
        hots.append(hot)
        work = jnp.where(hot, -jnp.inf, work)
    ex = [jnp.exp(v - vals[0]) for v in vals]
    den = ex[0] + ex[1] + ex[2] + ex[3]
    gates = [e / den for e in ex]

    msel = jnp.zeros(work.shape, F32)
    for hot in hots:
        msel = msel + jnp.where(hot, 1.0, 0.0)
    r_i = lax.broadcasted_iota(I32, (TM, TM), 0)
    c_i = lax.broadcasted_iota(I32, (TM, TM), 1)
    tri = jnp.where(r_i < c_i, 1.0, 0.0).astype(BF16)
    carry = carry_ref[:, 0:1]
    rank_full = jnp.dot(msel.astype(BF16), tri, preferred_element_type=F32) + carry
    ranks = [jnp.sum(jnp.where(hot, rank_full, 0.0), axis=0, keepdims=True) for hot in hots]
    new_carry = carry + jnp.sum(msel, axis=1, keepdims=True)
    carry_ref[...] = jnp.broadcast_to(new_carry, carry_ref.shape)
    cnt_ref[...] = jnp.broadcast_to(new_carry, cnt_ref.shape)

    row8 = lax.broadcasted_iota(I32, (8, TM), 0)

    def rows8(vs):
        out = jnp.zeros((8, TM), F32)
        for k, v in enumerate(vs):
            out = jnp.where(row8 == k, v, out)
        return out

    te_ref[...] = rows8(idxs).astype(I32)
    rk_ref[...] = rows8(ranks).astype(I32)
    g128 = jnp.concatenate([rows8(gates), jnp.zeros((LANES - 8, TM), F32)], axis=0)
    gt_ref[...] = g128.T


def _post_attention(oa, ob, oc, xa, mod3, g2, w_out_b, wr2, br, q0):
    b, t, d = xa.shape
    nt = t // TM - q0
    ntok = b * nt * TM
    tok = lambda w: pl.BlockSpec((1, TM, w), lambda bb, i: (bb, i, 0))
    const2 = lambda r, c: pl.BlockSpec((r, c), lambda bb, i: (0, 0))
    flat = lambda r: pl.BlockSpec((r, TM), lambda bb, i: (0, bb * nt + i))
    return pl.pallas_call(
        _post_kernel,
        grid=(b, nt),
        in_specs=[
            tok(A_Q), tok(B_V), tok(C_WIDTH),
            pl.BlockSpec((1, TM, d), lambda bb, i: (bb, i + q0, 0)),
            pl.BlockSpec((1, 1, mod3.shape[-1]), lambda bb, i: (jnp.where(i + q0 == 0, b, bb), 0, 0)),
            const2(1, d), const2(d, d), const2(d, 2 * LANES), const2(1, LANES),
        ],
        out_specs=[
            tok(d), tok(d), flat(8), flat(8),
            pl.BlockSpec((TM, LANES), lambda bb, i: (bb * nt + i, 0)),
            const2(N_EXPERTS, LANES),
        ],
        out_shape=[
            jax.ShapeDtypeStruct((b, nt * TM, d), F32),
            jax.ShapeDtypeStruct((b, nt * TM, d), F32),
            jax.ShapeDtypeStruct((8, ntok), I32),
            jax.ShapeDtypeStruct((8, ntok), I32),
            jax.ShapeDtypeStruct((ntok, LANES), F32),
            jax.ShapeDtypeStruct((N_EXPERTS, LANES), F32),
        ],
        scratch_shapes=[pltpu.VMEM((N_EXPERTS, LANES), F32)],
        compiler_params=_cparams(("arbitrary", "arbitrary"), VMEM_LIMIT),
        name="out_projection_router",
    )(oa, ob, oc, xa, mod3, g2, w_out_b, wr2, br)


def _row_copies(pos_ref, src_row, dst_row, sem):
    sublanes = 8

    def issue(i, carry):
        base = pl.multiple_of(i * sublanes, sublanes)
        for j in range(sublanes):
            for k in range(TOP_K):
                p = pos_ref[k * TM + base + j]
                pltpu.make_async_copy(src_row(k, base + j, p), dst_row(k, base + j, p), sem).start(priority=k % 2)
        return carry

    lax.fori_loop(0, TM // sublanes, issue, 0)


def _dispatch_kernel(pos_ref, tend_ref, f_ref, xs_ref, zbuf, sem, zsem):
    first = (pl.program_id(0) == 0) & (pl.program_id(1) == 0)

    @pl.when(first)
    def _():
        zbuf[...] = jnp.zeros_like(zbuf)

        def last_tile_copy(e):
            return pltpu.make_async_copy(zbuf, xs_ref.at[pl.ds((tend_ref[e] - 1) * E_TM, E_TM)], zsem)

        def has_tiles(e):
            return tend_ref[e] > (tend_ref[e - 1] if e else 0)

        def tail_copy(i):
            return pltpu.make_async_copy(zbuf, xs_ref.at[pl.ds(i * E_TM, E_TM)], zsem)

        n_used, n_tiles = tend_ref[N_EXPERTS - 1], xs_ref.shape[0] // E_TM
        for e in range(N_EXPERTS):
            pl.when(has_tiles(e))(lambda e=e: last_tile_copy(e).start())
        lax.fori_loop(n_used, n_tiles, lambda i, c: (tail_copy(i).start(), c)[1], 0)
        for e in range(N_EXPERTS):
            pl.when(has_tiles(e))(lambda e=e: last_tile_copy(e).wait())
        lax.fori_loop(n_used, n_tiles, lambda i, c: (tail_copy(i).wait(), c)[1], 0)

    _row_copies(pos_ref,
                lambda k, t, p: f_ref.at[0, pl.ds(t, 1)],
                lambda k, t, p: xs_ref.at[pl.ds(p, 1)], sem)
    for k in range(TOP_K):
        pltpu.make_async_copy(f_ref.at[0], xs_ref.at[pl.ds(0, TM)], sem).wait()


def _dispatch(pos_flat, tile_end, f, n_rows):
    b, t, hw = f.shape
    nt = t // TM
    return pl.pallas_call(
        _dispatch_kernel,
        grid=(b, nt),
        in_specs=[
            pl.BlockSpec((TOP_K * TM,), lambda bb, i: (bb * nt + i,), memory_space=pltpu.SMEM),
            pl.BlockSpec(memory_space=pltpu.SMEM),
            pl.BlockSpec((1, TM, hw), lambda bb, i: (bb, i, 0)),
        ],
        out_specs=pl.BlockSpec(memory_space=pl.ANY),
        out_shape=jax.ShapeDtypeStruct((n_rows, hw), F32),
        scratch_shapes=[pltpu.VMEM((E_TM, hw), F32), pltpu.SemaphoreType.DMA(()), pltpu.SemaphoreType.DMA(())],
        compiler_params=_cparams(("arbitrary", "arbitrary")),
        name="moe_dispatch",
    )(pos_flat, tile_end, f)


def _deinterleave_kernel(w_ref, p_ref, o_ref):
    fdim = w_ref.shape[2] // 2
    for j in range(fdim // LANES):
        chunk = w_ref[0, :, 2 * LANES * j:2 * LANES * (j + 1)].astype(BF16)
        r = jnp.dot(chunk, p_ref[...], preferred_element_type=F32)
        o_ref[0, :, LANES * j:LANES * (j + 1)] = r[:, 0:LANES].astype(BF16)
        o_ref[0, :, fdim + LANES * j:fdim + LANES * (j + 1)] = r[:, LANES:2 * LANES].astype(BF16)


def _deinterleave_weights(w, perm):
    n, d, f2 = w.shape
    tk = 512
    return pl.pallas_call(
        _deinterleave_kernel,
        grid=(n, d // tk),
        in_specs=[
            pl.BlockSpec((1, tk, f2), lambda e, j: (e, j, 0)),
            pl.BlockSpec(perm.shape, lambda e, j: (0, 0)),
        ],
        out_specs=pl.BlockSpec((1, tk, f2), lambda e, j: (e, j, 0)),
        out_shape=jax.ShapeDtypeStruct((n, d, f2), BF16),
        compiler_params=_cparams(("parallel", "parallel")),
        name="expert_weight_layout",
    )(w, perm)


def _experts_kernel(te_ref, tb_ref, nu_ref, x_ref, w1_ref, b1_ref, w2_ref, b2_ref, y_ref, w2b_ref):
    del tb_ref
    fdim = w2_ref.shape[1]
    i = pl.program_id(0)

    @pl.when(i >= nu_ref[0])
    def _():
        y_ref[...] = jnp.zeros_like(y_ref)

    @pl.when((i < nu_ref[0]) & ((i == 0) | (te_ref[i] != te_ref[jnp.maximum(i - 1, 0)])))
    def _():
        w2b_ref[...] = w2_ref[0].astype(BF16)

    @pl.when(i < nu_ref[0])
    def _():
        u = jnp.dot(x_ref[...].astype(BF16), w1_ref[0], preferred_element_type=F32) + b1_ref[0]
        glu = jnp.minimum(u[:, 0:fdim], SWIGLU_LIMIT)
        lin = jnp.clip(u[:, fdim:2 * fdim], -SWIGLU_LIMIT, SWIGLU_LIMIT)
        a = glu * _sigmoid(SWIGLU_ALPHA * glu) * (lin + 1.0)
        y_ref[...] = jnp.dot(a.astype(BF16), w2b_ref[...], preferred_element_type=F32) + b2_ref[0]


def _experts(tile_expert, tile_block, n_used, xs, w1p, b1p, w2, b2):
    n_rows, hw = xs.shape
    n_tiles = n_rows // E_TM
    _, d, f2 = w1p.shape
    fdim = f2 // 2
    grid_spec = pltpu.PrefetchScalarGridSpec(
        num_scalar_prefetch=3,
        grid=(n_tiles,),
        in_specs=[
            pl.BlockSpec((E_TM, hw), lambda i, te, tb, nu: (tb[i], 0)),
            pl.BlockSpec((1, d, f2), lambda i, te, tb, nu: (te[i], 0, 0)),
            pl.BlockSpec((1, 1, f2), lambda i, te, tb, nu: (te[i], 0, 0)),
            pl.BlockSpec((1, fdim, d), lambda i, te, tb, nu: (te[i], 0, 0)),
            pl.BlockSpec((1, 1, d), lambda i, te, tb, nu: (te[i], 0, 0)),
        ],
        out_specs=pl.BlockSpec((E_TM, hw), lambda i, te, tb, nu: (i, 0)),
        scratch_shapes=[pltpu.VMEM((fdim, d), BF16)],
    )
    return pl.pallas_call(
        _experts_kernel,
        grid_spec=grid_spec,
        out_shape=jax.ShapeDtypeStruct((n_rows, hw), F32),
        compiler_params=_cparams(("arbitrary",), 60 * 1024 * 1024),
        name="moe_experts",
    )(tile_expert, tile_block, n_used, xs, w1p, b1p, w2, b2)


def _combine_kernel(pos_ref, ys_ref, gt_ref, x_ref, mod_ref, gf_ref, o_ref, ybuf, sem, *, final):
    d = x_ref.shape[-1]
    _row_copies(pos_ref,
                lambda k, t, p: ys_ref.at[pl.ds(p, 1)],
                lambda k, t, p: ybuf.at[k, pl.ds(t, 1)], sem)
    for k in range(TOP_K):
        pltpu.make_async_copy(ys_ref.at[pl.ds(0, TM)], ybuf.at[k], sem).wait()
    acc = gt_ref[:, 0:1] * ybuf[0]
    for k in range(1, TOP_K):
        acc = acc + gt_ref[:, k:k + 1] * ybuf[k]
    xo = x_ref[0] + mod_ref[0, :, 5 * d:6 * d] * acc
    if final:
        xo = xo * lax.rsqrt(jnp.mean(xo * xo, axis=-1, keepdims=True) + EPS) * gf_ref[...]
    o_ref[0] = xo


def _combine(pos_flat, ys, gate_t, xmid, mod3, g_final, q0, final):
    b, t, d = xmid.shape
    nt = t // TM
    return pl.pallas_call(
        functools.partial(_combine_kernel, final=final),
        grid=(b, nt),
        in_specs=[
            pl.BlockSpec((TOP_K * TM,), lambda bb, i: (bb * nt + i,), memory_space=pltpu.SMEM),
            pl.BlockSpec(memory_space=pl.ANY),
            pl.BlockSpec((TM, LANES), lambda bb, i: (bb * nt + i, 0)),
            pl.BlockSpec((1, TM, d), lambda bb, i: (bb, i, 0)),
            pl.BlockSpec((1, 1, mod3.shape[-1]), lambda bb, i: (jnp.where(i + q0 == 0, b, bb), 0, 0)),
            pl.BlockSpec((1, d), lambda bb, i: (0, 0)),
        ],
        out_specs=pl.BlockSpec((1, TM, d), lambda bb, i: (bb, i, 0)),
        out_shape=jax.ShapeDtypeStruct((b, t, d), F32),
        scratch_shapes=[pltpu.VMEM((TOP_K, TM, d), F32), pltpu.SemaphoreType.DMA(())],
        compiler_params=_cparams(("arbitrary", "arbitrary"), VMEM_LIMIT),
        name="moe_combine",
    )(pos_flat, ys, gate_t, xmid, mod3, g_final)


def _routing_tables(top_e, rank, counts, n_tiles):
    cnt = counts[:, 0].astype(I32)
    tiles_e = (cnt + E_TM - 1) // E_TM
    tile_end = jnp.cumsum(tiles_e)
    row_start = (tile_end - tiles_e) * E_TM
    hit = top_e[0:TOP_K, :, None] == jnp.arange(N_EXPERTS, dtype=I32)
    pos = jnp.sum(jnp.where(hit, row_start, 0), axis=-1) + rank[0:TOP_K]
    ntok = pos.shape[1]
    pos_flat = pos.reshape(TOP_K, ntok // TM, TM).transpose(1, 0, 2).reshape(-1)
    n_used = tile_end[-1]
    ti = jnp.minimum(jnp.arange(n_tiles, dtype=I32), n_used - 1)
    tile_expert = jnp.minimum(jnp.sum((tile_end[None, :] <= ti[:, None]).astype(I32), axis=1), N_EXPERTS - 1)
    return pos_flat, tile_expert, ti, n_used.reshape(1).astype(I32), tile_end.astype(I32)


def _rope_tables(ctx_len, seq, head_dim, lanes_per_group):
    t = np.arange(seq)
    row, col = t // GRID_W, t % GRID_W
    quarter = head_dim // 4
    inv = ROPE_THETA ** (-np.arange(0, head_dim // 2, 2, dtype=np.float64) / (head_dim // 2))
    lane = np.arange(LANES) % lanes_per_group
    is_col = (lane // (head_dim // 2)) % 2 == 1
    w = lane % (head_dim // 2)
    freq = inv[w % quarter]
    pos = np.where(is_col[None, :], col[:, None], row[:, None]).astype(np.float64)
    ang = pos * freq[None, :]
    sign = np.where(w < quarter, -1.0, 1.0)[None, :]
    cos = np.concatenate([np.ones((ctx_len, LANES)), np.cos(ang)], axis=0)
    sin = np.concatenate([np.zeros((ctx_len, LANES)), np.sin(ang) * sign], axis=0)
    return jnp.asarray(cos, F32), jnp.asarray(sin, F32)


def _deinterleave_matrix():
    p = np.zeros((2 * LANES, 2 * LANES), np.float32)
    i = np.arange(LANES)
    p[2 * i, i] = 1.0
    p[2 * i + 1, LANES + i] = 1.0
    return jnp.asarray(p, BF16)


def kernel(x, c, ctx, c_ctx, w_mod, b_mod, g_norm1, g_norm2, w_in, g_qnorm, g_knorm, lambda_q1, lambda_k1,
           lambda_q2, lambda_k2, g_subln, w_pool, pool_scale, w_out, w_router, b_router, w_expert_in,
           b_expert_in, w_expert_out, b_expert_out, g_final):
    b, seq, d = x.shape
    ctx_len = ctx.shape[1]
    depth = w_mod.shape[0]
    assert ctx_len == TM and seq % TM == 0

    xa = jnp.concatenate([ctx, x], axis=1)
    cc = jnp.concatenate([c, c_ctx[None, :], jnp.zeros((8 - b - 1, d), F32)], axis=0)
    mods = _modulation(cc, w_mod, b_mod)
    rope_a = _rope_tables(ctx_len, seq, HEAD_DIM, HEAD_DIM)
    rope_b = _rope_tables(ctx_len, seq, B_QK_DIM, B_QK_DIM)
    e, _, f2 = w_expert_in.shape[1:]
    w1p_all = _deinterleave_weights(w_expert_in.reshape(depth * e, d, f2), _deinterleave_matrix())
    b1p_all = jnp.concatenate([b_expert_in[..., 0::2], b_expert_in[..., 1::2]], axis=-1).reshape(depth * e, 1, f2)
    w2_all = w_expert_out.reshape(depth * e, f2 // 2, d)
    b2_all = b_expert_out.reshape(depth * e, 1, d)

    for l in range(depth):
        last = l == depth - 1
        q0 = 1 if last else 0
        lam_init = 0.8 - 0.6 * math.exp(-0.3 * l)
        mod3 = mods[l].reshape(8, 1, 6 * d)
        qa, ka, vat, qb, kb, vbt, uc = _in_projection(
            xa, mod3, g_norm1[l][None], w_in[l].astype(BF16), g_qnorm[l][None], g_knorm[l][None],
            rope_a + rope_b)
        oa = _attention_a(qa, ka, vat, q0)
        lam_vecs = jnp.stack([lambda_q1[l], lambda_k1[l], lambda_q2[l], lambda_k2[l]])
        ob = _attention_b(qb, kb, vbt, lam_vecs, g_subln[l][:, None], q0, lam_init)
        oc = _pooling(uc, w_pool[l].astype(BF16), pool_scale[l][None], q0, ctx_len)

        wr = jnp.pad(w_router[l], ((0, 0), (0, LANES - e)))
        wr_hi = wr.astype(BF16)
        wr2 = jnp.concatenate([wr_hi, (wr - wr_hi.astype(F32)).astype(BF16)], axis=1)
        br = jnp.pad(b_router[l], (0, LANES - e))[None]
        xmid, f_moe, top_e, rank, gate_t, counts = _post_attention(
            oa, ob, oc, xa, mod3, g_norm2[l][None], w_out[l].astype(BF16), wr2, br, q0)

        ntok = top_e.shape[1]
        n_tiles = (TOP_K * ntok) // E_TM + e
        pos_flat, tile_expert, tile_block, n_used, tile_end = _routing_tables(top_e, rank, counts, n_tiles)
        xs = _dispatch(pos_flat, tile_end, f_moe, n_tiles * E_TM)
        ys = _experts(tile_expert + l * e, tile_block, n_used, xs, w1p_all, b1p_all, w2_all, b2_all)
        xa = _combine(pos_flat, ys, gate_t, xmid, mod3, g_final[None], q0, last)
    return xa
```

```python
import functools
import math

import numpy as np
import jax
import jax.numpy as jnp
from jax import lax
from jax.experimental import pallas as pl
from jax.experimental.pallas import tpu as pltpu

F32 = jnp.float32
BF16 = jnp.bfloat16
U32 = jnp.uint32
I32 = jnp.int32

HEAD_DIM = 128
A_HEADS = 8
A_KV_HEADS = 2
GQA_GROUP = A_HEADS // A_KV_HEADS
B_HEADS = 4
B_QK_DIM = 64
POOL_WINDOWS = (2, 4, 8, 16)
POOL_GROUP = 128
C_WIDTH = len(POOL_WINDOWS) * POOL_GROUP
A_Q = A_HEADS * HEAD_DIM
A_KV = A_KV_HEADS * HEAD_DIM
B_QK = B_HEADS * 2 * B_QK_DIM
B_V = B_HEADS * 2 * B_QK_DIM
N_EXPERTS = 32
TOP_K = 4
SWIGLU_LIMIT = 7.0
SWIGLU_ALPHA = 1.702
ROPE_THETA = 10000.0
GRID_W = 64
EPS = 1e-6

TM = 256
KV_TILE = 1024
BF16_ROWS = 16
VT_ROWS = 128 + BF16_ROWS
A_STREAM_HEADS = 2
E_TM = 512
POOL_HALO = 8
LANES = 128
SUBLANES = 8
VMEM_LIMIT = 56 * 1024 * 1024
LOG2E = math.log2(math.e)
NEG_BIG = -1e30


def _cparams(sem, vmem=None):
    return pltpu.CompilerParams(dimension_semantics=sem, vmem_limit_bytes=vmem)


def _sigmoid(z):
    return 1.0 / (1.0 + jnp.exp(-z))


def _mod_kernel(cc_ref, w_ref, b_ref, o_ref):
    cc = cc_ref[...]
    a = (cc * _sigmoid(cc)).astype(BF16)
    o_ref[0] = jnp.dot(a, w_ref[0].astype(BF16), preferred_element_type=F32) + b_ref[0]


def _modulation(cc, w_mod, b_mod):
    depth, d, n = w_mod.shape
    tn = 1536
    return pl.pallas_call(
        _mod_kernel,
        grid=(depth, n // tn),
        in_specs=[
            pl.BlockSpec((8, d), lambda l, j: (0, 0)),
            pl.BlockSpec((1, d, tn), lambda l, j: (l, 0, j)),
            pl.BlockSpec((1, 1, tn), lambda l, j: (l, 0, j)),
        ],
        out_specs=pl.BlockSpec((1, 8, tn), lambda l, j: (l, 0, j)),
        out_shape=jax.ShapeDtypeStruct((depth, 8, n), F32),
        compiler_params=_cparams(("parallel", "parallel"), VMEM_LIMIT),
        name="modulation",
    )(cc, w_mod, b_mod.reshape(depth, 1, n))


def _rope(y, cos, sin_signed, half):
    width = y.shape[-1]
    lane = lax.broadcasted_iota(I32, y.shape, 1)
    partner = jnp.where(lane % (2 * half) < half,
                        pltpu.roll(y, width - half, 1), pltpu.roll(y, half, 1))
    return y * cos + partner * sin_signed


def _head_rms(y, g):
    return y * lax.rsqrt(jnp.mean(y * y, axis=-1, keepdims=True) + EPS) * g


def _proj_kernel(x_ref, mod_ref, g1_ref, w_ref, gq_ref, gk_ref, ca_ref, sa_ref, cb_ref, sb_ref,
                 qa_ref, ka_ref, vat_ref, qb_ref, kb_ref, vbt_ref, uc_ref):
    d = x_ref.shape[-1]
    x = x_ref[0]
    xn = x * lax.rsqrt(jnp.mean(x * x, axis=-1, keepdims=True) + EPS) * g1_ref[...]
    sh = mod_ref[0, :, 0:d]
    sc = mod_ref[0, :, d:2 * d]
    h = (xn * (1.0 + sc) + sh).astype(BF16)
    ca, sa, cb, sb = ca_ref[...], sa_ref[...], cb_ref[...], sb_ref[...]

    qa_scale = LOG2E / math.sqrt(HEAD_DIM)
    qb_scale = LOG2E / math.sqrt(B_QK_DIM)
    ones = jnp.ones((BF16_ROWS, x.shape[0]), BF16)

    def store_vt(ref, hd, v):
        ref[0, hd * VT_ROWS:hd * VT_ROWS + HEAD_DIM, :] = v.T.astype(BF16)
        ref[0, hd * VT_ROWS + HEAD_DIM:(hd + 1) * VT_ROWS, :] = ones

    c0 = 0
    pq = jnp.dot(h, w_ref[:, c0:c0 + A_Q], preferred_element_type=F32)
    for hd in range(A_HEADS):
        y = _head_rms(pq[:, hd * HEAD_DIM:(hd + 1) * HEAD_DIM], gq_ref[...])
        qa_ref[0, :, hd * HEAD_DIM:(hd + 1) * HEAD_DIM] = (
            _rope(y, ca, sa, HEAD_DIM // 4) * qa_scale).astype(BF16)
    c0 += A_Q
    pkv = jnp.dot(h, w_ref[:, c0:c0 + 2 * A_KV], preferred_element_type=F32)
    for hd in range(A_KV_HEADS):
        y = _head_rms(pkv[:, hd * HEAD_DIM:(hd + 1) * HEAD_DIM], gk_ref[...])
        ka_ref[0, :, hd * HEAD_DIM:(hd + 1) * HEAD_DIM] = _rope(y, ca, sa, HEAD_DIM // 4).astype(BF16)
        store_vt(vat_ref, hd, pkv[:, A_KV + hd * HEAD_DIM:A_KV + (hd + 1) * HEAD_DIM])
    c0 += 2 * A_KV
    pb = jnp.dot(h, w_ref[:, c0:c0 + 2 * B_QK], preferred_element_type=F32)
    lane = lax.broadcasted_iota(I32, (x.shape[0], LANES), 1)

    def head_pair(base, hd):
        v0 = pb[:, base + (hd // 2) * LANES:base + (hd // 2 + 1) * LANES]
        v1 = pb[:, base + B_QK // 2 + (hd // 2) * LANES:base + B_QK // 2 + (hd // 2 + 1) * LANES]
        if hd % 2 == 0:
            return jnp.where(lane < B_QK_DIM, v0, pltpu.roll(v1, B_QK_DIM, 1))
        return jnp.where(lane < B_QK_DIM, pltpu.roll(v0, B_QK_DIM, 1), v1)

    for hd in range(B_HEADS):
        sl = slice(hd * LANES, (hd + 1) * LANES)
        qb_ref[0, :, sl] = (_rope(head_pair(0, hd), cb, sb, B_QK_DIM // 4) * qb_scale).astype(BF16)
        kb_ref[0, :, sl] = _rope(head_pair(B_QK, hd), cb, sb, B_QK_DIM // 4).astype(BF16)
    c0 += 2 * B_QK
    pvu = jnp.dot(h, w_ref[:, c0:c0 + B_V + C_WIDTH], preferred_element_type=F32)
    for hd in range(B_HEADS):
        store_vt(vbt_ref, hd, pvu[:, hd * HEAD_DIM:(hd + 1) * HEAD_DIM])
    uc_ref[0] = pvu[:, B_V:B_V + C_WIDTH]


def _in_projection(xa, mod3, g1, w_in_p, gq, gk, rope):
    b, t, d = xa.shape
    nt = t // TM
    n_in = w_in_p.shape[1]
    tok = lambda w: pl.BlockSpec((1, TM, w), lambda bb, i: (bb, i, 0))
    tok_t = lambda w: pl.BlockSpec((1, w, TM), lambda bb, i: (bb, 0, i))
    const2 = lambda r, c: pl.BlockSpec((r, c), lambda bb, i: (0, 0))
    tab = pl.BlockSpec((TM, LANES), lambda bb, i: (i, 0))
    return pl.pallas_call(
        _proj_kernel,
        grid=(b, nt),
        in_specs=[
            tok(d),
            pl.BlockSpec((1, 1, mod3.shape[-1]), lambda bb, i: (jnp.where(i == 0, b, bb), 0, 0)),
            const2(1, d), const2(d, n_in), const2(1, HEAD_DIM), const2(1, HEAD_DIM),
            tab, tab, tab, tab,
        ],
        out_specs=[tok(A_Q), tok(A_KV), tok_t(A_KV_HEADS * VT_ROWS), tok(B_QK), tok(B_QK),
                   tok_t(B_HEADS * VT_ROWS), tok(C_WIDTH)],
        out_shape=[
            jax.ShapeDtypeStruct((b, t, A_Q), BF16),
            jax.ShapeDtypeStruct((b, t, A_KV), BF16),
            jax.ShapeDtypeStruct((b, A_KV_HEADS * VT_ROWS, t), BF16),
            jax.ShapeDtypeStruct((b, t, B_QK), BF16),
            jax.ShapeDtypeStruct((b, t, B_QK), BF16),
            jax.ShapeDtypeStruct((b, B_HEADS * VT_ROWS, t), BF16),
            jax.ShapeDtypeStruct((b, t, C_WIDTH), F32),
        ],
        compiler_params=_cparams(("parallel", "parallel"), VMEM_LIMIT),
        name="in_projection",
    )(xa, mod3, g1, w_in_p, gq, gk, *rope)


def _flash_t(q_stacks, kv_of, k_ref, vt_ref, m_ref, acc_ref, n_kv):
    def step(off, size):
        ss = [lax.dot_general(k_ref[0, pl.ds(off, size), kv * LANES:(kv + 1) * LANES], q,
                              (((1,), (1,)), ((), ())), preferred_element_type=F32)
              for kv, q in zip(kv_of, q_stacks)]
        for h, (kv, s) in enumerate(zip(kv_of, ss)):
            m = m_ref[h]
            m_new = jnp.maximum(m, jnp.max(s, axis=0, keepdims=True))
            p = jnp.exp2(s - m_new)
            vt = vt_ref[0, kv * VT_ROWS:(kv + 1) * VT_ROWS, pl.ds(off, size)]
            acc_ref[h] = (jnp.exp2(m - m_new) * acc_ref[h]
                          + jnp.dot(vt, p.astype(BF16), preferred_element_type=F32))
            m_ref[h] = m_new

    m_ref[...] = jnp.full(m_ref.shape, NEG_BIG, F32)
    acc_ref[...] = jnp.zeros(acc_ref.shape, F32)
    step(0, TM)

    def body(j, carry):
        step(pl.multiple_of(TM + j * KV_TILE, TM), KV_TILE)
        return carry

    lax.fori_loop(0, n_kv, body, 0)
    return [acc_ref[h, 0:HEAD_DIM] / acc_ref[h, HEAD_DIM:HEAD_DIM + 1] for h in range(len(q_stacks))]


def _attn_a_kernel(q_ref, k_ref, vt_ref, o_ref, m_ref, acc_ref, *, q0, n_kv_full):
    qi = pl.program_id(1) + q0
    n_kv = jnp.where(qi == 0, 0, n_kv_full)
    q = q_ref[0]
    n_streams = A_HEADS // A_STREAM_HEADS
    q_stacks = [jnp.concatenate([q[:, (s * A_STREAM_HEADS + g) * HEAD_DIM:(s * A_STREAM_HEADS + g + 1) * HEAD_DIM]
                                 for g in range(A_STREAM_HEADS)], axis=0) for s in range(n_streams)]
    kv_of = [s * A_STREAM_HEADS // GQA_GROUP for s in range(n_streams)]
    for s, o in enumerate(_flash_t(q_stacks, kv_of, k_ref, vt_ref, m_ref, acc_ref, n_kv)):
        for g in range(A_STREAM_HEADS):
            c0 = (s * A_STREAM_HEADS + g) * HEAD_DIM
            o_ref[0, :, c0:c0 + HEAD_DIM] = o[:, g * TM:(g + 1) * TM].T.astype(BF16)


def _attn_b_kernel(q_ref, k_ref, vt_ref, lam_ref, gs_ref, o_ref, m_ref, acc_ref, *, q0, n_kv_full, lam_init):
    qi = pl.program_id(1) + q0
    n_kv = jnp.where(qi == 0, 0, n_kv_full)
    lane = lax.broadcasted_iota(I32, (TM, LANES), 1)
    zero = jnp.zeros((TM, LANES), BF16)
    q_stacks = []
    for h in range(B_HEADS):
        q = q_ref[0, :, h * LANES:(h + 1) * LANES]
        q_stacks.append(jnp.concatenate(
            [jnp.where(lane < B_QK_DIM, q, zero), jnp.where(lane >= B_QK_DIM, q, zero)], axis=0))
    res = _flash_t(q_stacks, list(range(B_HEADS)), k_ref, vt_ref, m_ref, acc_ref, n_kv)
    lv = lam_ref[...]
    lam = (jnp.exp(jnp.sum(lv[0:1] * lv[1:2], axis=-1, keepdims=True))
           - jnp.exp(jnp.sum(lv[2:3] * lv[3:4], axis=-1, keepdims=True)) + lam_init)
    for h, o in enumerate(res):
        od = o[:, 0:TM] - lam * o[:, TM:2 * TM]
        r = lax.rsqrt(jnp.mean(od * od, axis=0, keepdims=True) + EPS)
        y = od * r * gs_ref[...] * (1.0 - lam_init)
        o_ref[0, :, h * LANES:(h + 1) * LANES] = y.T.astype(BF16)


def _attention_a(qa, ka, vat, q0):
    b, t, _ = qa.shape
    nt = t // TM
    return pl.pallas_call(
        functools.partial(_attn_a_kernel, q0=q0, n_kv_full=(t - TM) // KV_TILE),
        grid=(b, nt - q0),
        in_specs=[
            pl.BlockSpec((1, TM, A_Q), lambda bb, i: (bb, i + q0, 0)),
            pl.BlockSpec((1, t, A_KV), lambda bb, i: (bb, 0, 0)),
            pl.BlockSpec((1, A_KV_HEADS * VT_ROWS, t), lambda bb, i: (bb, 0, 0)),
        ],
        out_specs=pl.BlockSpec((1, TM, A_Q), lambda bb, i: (bb, i, 0)),
        out_shape=jax.ShapeDtypeStruct((b, t - q0 * TM, A_Q), BF16),
        scratch_shapes=[pltpu.VMEM((A_HEADS // A_STREAM_HEADS, 1, A_STREAM_HEADS * TM), F32),
                        pltpu.VMEM((A_HEADS // A_STREAM_HEADS, VT_ROWS, A_STREAM_HEADS * TM), F32)],
        compiler_params=_cparams(("parallel", "parallel"), VMEM_LIMIT),
        name="attention_gqa",
    )(qa, ka, vat)


def _attention_b(qb, kb, vbt, lam_vecs, g_sub_col, q0, lam_init):
    b, t, _ = qb.shape
    nt = t // TM
    return pl.pallas_call(
        functools.partial(_attn_b_kernel, q0=q0, n_kv_full=(t - TM) // KV_TILE, lam_init=lam_init),
        grid=(b, nt - q0),
        in_specs=[
            pl.BlockSpec((1, TM, B_QK), lambda bb, i: (bb, i + q0, 0)),
            pl.BlockSpec((1, t, B_QK), lambda bb, i: (bb, 0, 0)),
            pl.BlockSpec((1, B_HEADS * VT_ROWS, t), lambda bb, i: (bb, 0, 0)),
            pl.BlockSpec((4, B_QK_DIM), lambda bb, i: (0, 0)),
            pl.BlockSpec((LANES, 1), lambda bb, i: (0, 0)),
        ],
        out_specs=pl.BlockSpec((1, TM, B_V), lambda bb, i: (bb, i, 0)),
        out_shape=jax.ShapeDtypeStruct((b, t - q0 * TM, B_V), BF16),
        scratch_shapes=[pltpu.VMEM((B_HEADS, 1, 2 * TM), F32), pltpu.VMEM((B_HEADS, VT_ROWS, 2 * TM), F32)],
        compiler_params=_cparams(("parallel", "parallel"), VMEM_LIMIT),
        name="attention_diff",
    )(qb, kb, vbt, lam_vecs, g_sub_col)


def _pool_kernel(prev_ref, cur_ref, next_ref, wp_ref, ps_ref, o_ref, *, q0, ctx_len, total_len):
    i = pl.program_id(1) + q0
    ext = jnp.concatenate([prev_ref[0], cur_ref[0], next_ref[0]], axis=0)
    rows = ext.shape[0]
    seg_lo = jnp.where(i == 0, 0, ctx_len)
    seg_hi = jnp.where(i == 0, ctx_len, total_len)
    grow = i * TM - POOL_HALO + lax.broadcasted_iota(I32, (rows, 1), 0)
    ext = jnp.where((grow >= seg_lo) & (grow < seg_hi), ext, 0.0)
    tpos = grow[POOL_HALO:POOL_HALO + TM] - seg_lo
    seg_len = seg_hi - seg_lo

    def back(a, d):
        return pltpu.roll(a, d, 0)

    def fwd(a, d):
        return pltpu.roll(a, rows - d, 0)

    for gi, w in enumerate(POOL_WINDOWS):
        u = ext[:, gi * POOL_GROUP:(gi + 1) * POOL_GROUP]
        acc = u + back(u, 1)
        span = 2
        while span < w:
            acc = acc + back(acc, span)
            span *= 2
        if w > 2:
            acc = fwd(acc, w // 2 - 1)
        cnt = (jnp.minimum(tpos + w // 2, seg_len) - jnp.maximum(tpos - w // 2, 0)).astype(F32)
        sl = slice(POOL_HALO, POOL_HALO + TM)
        p = (acc[sl] / cnt - u[sl]).astype(BF16)
        y = jnp.dot(p, wp_ref[gi], preferred_element_type=F32)
        o_ref[0, :, gi * POOL_GROUP:(gi + 1) * POOL_GROUP] = (
            y * ps_ref[:, gi * POOL_GROUP:(gi + 1) * POOL_GROUP]).astype(BF16)


def _pooling(uc, w_pool_b, pool_scale, q0, ctx_len):
    b, t, cw = uc.shape
    nt = t // TM
    per = TM // POOL_HALO
    last = t // POOL_HALO - 1
    return pl.pallas_call(
        functools.partial(_pool_kernel, q0=q0, ctx_len=ctx_len, total_len=t),
        grid=(b, nt - q0),
        in_specs=[
            pl.BlockSpec((1, POOL_HALO, cw), lambda bb, i: (bb, jnp.maximum((i + q0) * per - 1, 0), 0)),
            pl.BlockSpec((1, TM, cw), lambda bb, i: (bb, i + q0, 0)),
            pl.BlockSpec((1, POOL_HALO, cw), lambda bb, i: (bb, jnp.minimum((i + q0 + 1) * per, last), 0)),
            pl.BlockSpec(w_pool_b.shape, lambda bb, i: (0, 0, 0)),
            pl.BlockSpec((1, cw), lambda bb, i: (0, 0)),
        ],
        out_specs=pl.BlockSpec((1, TM, cw), lambda bb, i: (bb, i, 0)),
        out_shape=jax.ShapeDtypeStruct((b, t - q0 * TM, cw), BF16),
        compiler_params=_cparams(("parallel", "parallel")),
        name="pooling",
    )(uc, uc, uc, w_pool_b, pool_scale)


def _post_kernel(oa_ref, ob_ref, oc_ref, x_ref, mod_ref, g2_ref, wo_ref, wr_ref, br_ref,
                 xmid_ref, f_ref, te_ref, rk_ref, gt_ref, cnt_ref, carry_ref):
    d = x_ref.shape[-1]
    first = (pl.program_id(0) == 0) & (pl.program_id(1) == 0)

    @pl.when(first)
    def _():
        carry_ref[...] = jnp.zeros_like(carry_ref)

    m = (jnp.dot(oa_ref[0], wo_ref[0:A_Q], preferred_element_type=F32)
         + jnp.dot(ob_ref[0], wo_ref[A_Q:A_Q + B_V], preferred_element_type=F32)
         + jnp.dot(oc_ref[0], wo_ref[A_Q + B_V:A_Q + B_V + C_WIDTH], preferred_element_type=F32))
    x = x_ref[0] + mod_ref[0, :, 2 * d:3 * d] * m
    xmid_ref[0] = x
    xn = x * lax.rsqrt(jnp.mean(x * x, axis=-1, keepdims=True) + EPS) * g2_ref[...]
    f = xn * (1.0 + mod_ref[0, :, 4 * d:5 * d]) + mod_ref[0, :, 3 * d:4 * d]
    f_ref[0] = f

    f_hi = f.astype(BF16)
    f_lo = (f - f_hi.astype(F32)).astype(BF16)
    lg2 = jnp.dot(f_hi, wr_ref[...], preferred_element_type=F32)
    lg1 = jnp.dot(f_lo, wr_ref[:, 0:LANES], preferred_element_type=F32)
    logits = lg2[:, 0:LANES] + lg2[:, LANES:2 * LANES] + lg1 + br_ref[...]
    work = logits.T[0:N_EXPERTS]

    e_id = lax.broadcasted_iota(I32, work.shape, 0).astype(F32)
    vals, idxs, hots = [], [], []
    for _ in range(TOP_K):
        mx = jnp.max(work, axis=0, keepdims=True)
        idx = jnp.min(jnp.where(work == mx, e_id, float(N_EXPERTS)), axis=0, keepdims=True)
        hot = e_id == idx
        vals.append(mx)
        idxs.append(idx)
        hots.append(hot)
        work = jnp.where(hot, -jnp.inf, work)
    ex = [jnp.exp(v - vals[0]) for v in vals]
    den = ex[0] + ex[1] + ex[2] + ex[3]
    gates = [e / den for e in ex]

    msel = jnp.zeros(work.shape, F32)
    for hot in hots:
        msel = msel + jnp.where(hot, 1.0, 0.0)
    r_i = lax.broadcasted_iota(I32, (TM, TM), 0)
    c_i = lax.broadcasted_iota(I32, (TM, TM), 1)
    tri = jnp.where(r_i < c_i, 1.0, 0.0).astype(BF16)
    carry = carry_ref[:, 0:1]
    rank_full = jnp.dot(msel.astype(BF16), tri, preferred_element_type=F32) + carry
    ranks = [jnp.sum(jnp.where(hot, rank_full, 0.0), axis=0, keepdims=True) for hot in hots]
    new_carry = carry + jnp.sum(msel, axis=1, keepdims=True)
    carry_ref[...] = jnp.broadcast_to(new_carry, carry_ref.shape)
    cnt_ref[...] = jnp.broadcast_to(new_carry, cnt_ref.shape)

    row8 = lax.broadcasted_iota(I32, (8, TM), 0)

    def rows8(vs):
        out = jnp.zeros((8, TM), F32)
        for k, v in enumerate(vs):
            out = jnp.where(row8 == k, v, out)
        return out

    te_ref[...] = rows8(idxs).astype(I32)
    rk_ref[...] = rows8(ranks).astype(I32)
    g128 = jnp.concatenate([rows8(gates), jnp.zeros((LANES - 8, TM), F32)], axis=0)
    gt_ref[...] = g128.T


def _post_attention(oa, ob, oc, xa, mod3, g2, w_out_b, wr2, br, q0):
    b, t, d = xa.shape
    nt = t // TM - q0
    ntok = b * nt * TM
    tok = lambda w: pl.BlockSpec((1, TM, w), lambda bb, i: (bb, i, 0))
    const2 = lambda r, c: pl.BlockSpec((r, c), lambda bb, i: (0, 0))
    flat = lambda r: pl.BlockSpec((r, TM), lambda bb, i: (0, bb * nt + i))
    return pl.pallas_call(
        _post_kernel,
        grid=(b, nt),
        in_specs=[
            tok(A_Q), tok(B_V), tok(C_WIDTH),
            pl.BlockSpec((1, TM, d), lambda bb, i: (bb, i + q0, 0)),
            pl.BlockSpec((1, 1, mod3.shape[-1]), lambda bb, i: (jnp.where(i + q0 == 0, b, bb), 0, 0)),
            const2(1, d), const2(d, d), const2(d, 2 * LANES), const2(1, LANES),
        ],
        out_specs=[
            tok(d), tok(d), flat(8), flat(8),
            pl.BlockSpec((TM, LANES), lambda bb, i: (bb * nt + i, 0)),
            const2(N_EXPERTS, LANES),
        ],
        out_shape=[
            jax.ShapeDtypeStruct((b, nt * TM, d), F32),
            jax.ShapeDtypeStruct((b, nt * TM, d), F32),
            jax.ShapeDtypeStruct((8, ntok), I32),
            jax.ShapeDtypeStruct((8, ntok), I32),
            jax.ShapeDtypeStruct((ntok, LANES), F32),
            jax.ShapeDtypeStruct((N_EXPERTS, LANES), F32),
        ],
        scratch_shapes=[pltpu.VMEM((N_EXPERTS, LANES), F32)],
        compiler_params=_cparams(("arbitrary", "arbitrary"), VMEM_LIMIT),
        name="out_projection_router",
    )(oa, ob, oc, xa, mod3, g2, w_out_b, wr2, br)


def _row_copies(pos_ref, src_row, dst_row, sem):
    def issue(i, carry):
        base = pl.multiple_of(i * SUBLANES, SUBLANES)
        for j in range(SUBLANES):
            for k in range(TOP_K):
                p = pos_ref[k * TM + base + j]
                pltpu.make_async_copy(src_row(k, base + j, p), dst_row(k, base + j, p), sem).start(priority=k % 2)
        return carry

    lax.fori_loop(0, TM // SUBLANES, issue, 0)


def _dispatch_kernel(pos_ref, tend_ref, f_ref, xs_ref, zbuf, sem, zsem):
    first = (pl.program_id(0) == 0) & (pl.program_id(1) == 0)

    @pl.when(first)
    def _():
        zbuf[...] = jnp.zeros_like(zbuf)

        def last_tile_copy(e):
            return pltpu.make_async_copy(zbuf, xs_ref.at[pl.ds((tend_ref[e] - 1) * E_TM, E_TM)], zsem)

        def has_tiles(e):
            return tend_ref[e] > (tend_ref[e - 1] if e else 0)

        def tail_copy(i):
            return pltpu.make_async_copy(zbuf, xs_ref.at[pl.ds(i * E_TM, E_TM)], zsem)

        n_used, n_tiles = tend_ref[N_EXPERTS - 1], xs_ref.shape[0] // E_TM
        for e in range(N_EXPERTS):
            pl.when(has_tiles(e))(lambda e=e: last_tile_copy(e).start())
        lax.fori_loop(n_used, n_tiles, lambda i, c: (tail_copy(i).start(), c)[1], 0)
        for e in range(N_EXPERTS):
            pl.when(has_tiles(e))(lambda e=e: last_tile_copy(e).wait())
        lax.fori_loop(n_used, n_tiles, lambda i, c: (tail_copy(i).wait(), c)[1], 0)

    _row_copies(pos_ref,
                lambda k, t, p: f_ref.at[0, pl.ds(t, 1)],
                lambda k, t, p: xs_ref.at[pl.ds(p, 1)], sem)
    for k in range(TOP_K):
        pltpu.make_async_copy(f_ref.at[0], xs_ref.at[pl.ds(0, TM)], sem).wait()


def _dispatch(pos_flat, tile_end, f, n_rows):
    b, t, hw = f.shape
    nt = t // TM
    return pl.pallas_call(
        _dispatch_kernel,
        grid=(b, nt),
        in_specs=[
            pl.BlockSpec((TOP_K * TM,), lambda bb, i: (bb * nt + i,), memory_space=pltpu.SMEM),
            pl.BlockSpec(memory_space=pltpu.SMEM),
            pl.BlockSpec((1, TM, hw), lambda bb, i: (bb, i, 0)),
        ],
        out_specs=pl.BlockSpec(memory_space=pl.ANY),
        out_shape=jax.ShapeDtypeStruct((n_rows, hw), F32),
        scratch_shapes=[pltpu.VMEM((E_TM, hw), F32), pltpu.SemaphoreType.DMA(()), pltpu.SemaphoreType.DMA(())],
        compiler_params=_cparams(("arbitrary", "arbitrary")),
        name="moe_dispatch",
    )(pos_flat, tile_end, f)


def _deinterleave_kernel(w_ref, p_ref, o_ref):
    fdim = w_ref.shape[2] // 2
    for j in range(fdim // LANES):
        chunk = w_ref[0, :, 2 * LANES * j:2 * LANES * (j + 1)].astype(BF16)
        r = jnp.dot(chunk, p_ref[...], preferred_element_type=F32)
        o_ref[0, :, LANES * j:LANES * (j + 1)] = r[:, 0:LANES].astype(BF16)
        o_ref[0, :, fdim + LANES * j:fdim + LANES * (j + 1)] = r[:, LANES:2 * LANES].astype(BF16)


def _deinterleave_weights(w, perm):
    n, d, f2 = w.shape
    tk = 512
    return pl.pallas_call(
        _deinterleave_kernel,
        grid=(n, d // tk),
        in_specs=[
            pl.BlockSpec((1, tk, f2), lambda e, j: (e, j, 0)),
            pl.BlockSpec(perm.shape, lambda e, j: (0, 0)),
        ],
        out_specs=pl.BlockSpec((1, tk, f2), lambda e, j: (e, j, 0)),
        out_shape=jax.ShapeDtypeStruct((n, d, f2), BF16),
        compiler_params=_cparams(("parallel", "parallel")),
        name="expert_weight_layout",
    )(w, perm)


def _experts_kernel(te_ref, tb_ref, rows_ref, x_ref, w1_ref, b1_ref, w2_ref, b2_ref, y_ref, w2b_ref):
    del tb_ref
    fdim = w2_ref.shape[1]
    half = E_TM // 2
    i = pl.program_id(0)
    rows = rows_ref[i]

    @pl.when(rows == 0)
    def _():
        y_ref[...] = jnp.zeros_like(y_ref)

    @pl.when((rows > 0) & ((i == 0) | (te_ref[i] != te_ref[jnp.maximum(i - 1, 0)])))
    def _():
        w2b_ref[...] = w2_ref[0].astype(BF16)

    def swiglu_mlp(n):
        u = jnp.dot(x_ref[0:n, :].astype(BF16), w1_ref[0], preferred_element_type=F32) + b1_ref[0]
        glu = jnp.minimum(u[:, 0:fdim], SWIGLU_LIMIT)
        lin = jnp.clip(u[:, fdim:2 * fdim], -SWIGLU_LIMIT, SWIGLU_LIMIT)
        a = glu * _sigmoid(SWIGLU_ALPHA * glu) * (lin + 1.0)
        y_ref[0:n, :] = jnp.dot(a.astype(BF16), w2b_ref[...], preferred_element_type=F32) + b2_ref[0]

    @pl.when(rows > half)
    def _():
        swiglu_mlp(E_TM)

    @pl.when((rows > 0) & (rows <= half))
    def _():
        swiglu_mlp(half)
        y_ref[half:E_TM, :] = jnp.zeros((E_TM - half, y_ref.shape[1]), F32)


def _experts(tile_expert, tile_block, tile_rows, xs, w1p, b1p, w2, b2):
    n_rows, hw = xs.shape
    n_tiles = n_rows // E_TM
    _, d, f2 = w1p.shape
    fdim = f2 // 2
    grid_spec = pltpu.PrefetchScalarGridSpec(
        num_scalar_prefetch=3,
        grid=(n_tiles,),
        in_specs=[
            pl.BlockSpec((E_TM, hw), lambda i, te, tb, nu: (tb[i], 0)),
            pl.BlockSpec((1, d, f2), lambda i, te, tb, nu: (te[i], 0, 0)),
            pl.BlockSpec((1, 1, f2), lambda i, te, tb, nu: (te[i], 0, 0)),
            pl.BlockSpec((1, fdim, d), lambda i, te, tb, nu: (te[i], 0, 0)),
            pl.BlockSpec((1, 1, d), lambda i, te, tb, nu: (te[i], 0, 0)),
        ],
        out_specs=pl.BlockSpec((E_TM, hw), lambda i, te, tb, nu: (i, 0)),
        scratch_shapes=[pltpu.VMEM((fdim, d), BF16)],
    )
    return pl.pallas_call(
        _experts_kernel,
        grid_spec=grid_spec,
        out_shape=jax.ShapeDtypeStruct((n_rows, hw), F32),
        compiler_params=_cparams(("arbitrary",), 60 * 1024 * 1024),
        name="moe_experts",
    )(tile_expert, tile_block, tile_rows, xs, w1p, b1p, w2, b2)


def _combine_kernel(pos_ref, ys_ref, gt_ref, x_ref, mod_ref, gf_ref, o_ref, ybuf, sem, *, final):
    d = x_ref.shape[-1]
    _row_copies(pos_ref,
                lambda k, t, p: ys_ref.at[pl.ds(p, 1)],
                lambda k, t, p: ybuf.at[k, pl.ds(t, 1)], sem)
    for k in range(TOP_K):
        pltpu.make_async_copy(ys_ref.at[pl.ds(0, TM)], ybuf.at[k], sem).wait()
    acc = gt_ref[:, 0:1] * ybuf[0]
    for k in range(1, TOP_K):
        acc = acc + gt_ref[:, k:k + 1] * ybuf[k]
    xo = x_ref[0] + mod_ref[0, :, 5 * d:6 * d] * acc
    if final:
        xo = xo * lax.rsqrt(jnp.mean(xo * xo, axis=-1, keepdims=True) + EPS) * gf_ref[...]
    o_ref[0] = xo


def _combine(pos_flat, ys, gate_t, xmid, mod3, g_final, q0, final):
    b, t, d = xmid.shape
    nt = t // TM
    return pl.pallas_call(
        functools.partial(_combine_kernel, final=final),
        grid=(b, nt),
        in_specs=[
            pl.BlockSpec((TOP_K * TM,), lambda bb, i: (bb * nt + i,), memory_space=pltpu.SMEM),
            pl.BlockSpec(memory_space=pl.ANY),
            pl.BlockSpec((TM, LANES), lambda bb, i: (bb * nt + i, 0)),
            pl.BlockSpec((1, TM, d), lambda bb, i: (bb, i, 0)),
            pl.BlockSpec((1, 1, mod3.shape[-1]), lambda bb, i: (jnp.where(i + q0 == 0, b, bb), 0, 0)),
            pl.BlockSpec((1, d), lambda bb, i: (0, 0)),
        ],
        out_specs=pl.BlockSpec((1, TM, d), lambda bb, i: (bb, i, 0)),
        out_shape=jax.ShapeDtypeStruct((b, t, d), F32),
        scratch_shapes=[pltpu.VMEM((TOP_K, TM, d), F32), pltpu.SemaphoreType.DMA(())],
        compiler_params=_cparams(("arbitrary", "arbitrary"), VMEM_LIMIT),
        name="moe_combine",
    )(pos_flat, ys, gate_t, xmid, mod3, g_final)


def _routing_tables(top_e, rank, counts, n_tiles):
    cnt = counts[:, 0].astype(I32)
    tiles_e = (cnt + E_TM - 1) // E_TM
    tile_end = jnp.cumsum(tiles_e)
    row_start = (tile_end - tiles_e) * E_TM
    hit = top_e[0:TOP_K, :, None] == jnp.arange(N_EXPERTS, dtype=I32)
    pos = jnp.sum(jnp.where(hit, row_start, 0), axis=-1) + rank[0:TOP_K]
    ntok = pos.shape[1]
    pos_flat = pos.reshape(TOP_K, ntok // TM, TM).transpose(1, 0, 2).reshape(-1)
    n_used = tile_end[-1]
    tile_id = jnp.arange(n_tiles, dtype=I32)
    ti = jnp.minimum(tile_id, n_used - 1)
    tile_expert = jnp.minimum(jnp.sum((tile_end[None, :] <= ti[:, None]).astype(I32), axis=1), N_EXPERTS - 1)
    mine = tile_expert[:, None] == jnp.arange(N_EXPERTS, dtype=I32)
    rows_left = jnp.sum(jnp.where(mine, cnt + row_start, 0), axis=1) - ti * E_TM
    tile_rows = jnp.where(tile_id < n_used, jnp.clip(rows_left, 0, E_TM), 0).astype(I32)
    return pos_flat, tile_expert, ti, tile_rows, tile_end.astype(I32)


def _rope_tables(ctx_len, seq, head_dim, lanes_per_group):
    t = np.arange(seq)
    row, col = t // GRID_W, t % GRID_W
    quarter = head_dim // 4
    inv = ROPE_THETA ** (-np.arange(0, head_dim // 2, 2, dtype=np.float64) / (head_dim // 2))
    lane = np.arange(LANES) % lanes_per_group
    is_col = (lane // (head_dim // 2)) % 2 == 1
    w = lane % (head_dim // 2)
    freq = inv[w % quarter]
    pos = np.where(is_col[None, :], col[:, None], row[:, None]).astype(np.float64)
    ang = pos * freq[None, :]
    sign = np.where(w < quarter, -1.0, 1.0)[None, :]
    cos = np.concatenate([np.ones((ctx_len, LANES)), np.cos(ang)], axis=0)
    sin = np.concatenate([np.zeros((ctx_len, LANES)), np.sin(ang) * sign], axis=0)
    return jnp.asarray(cos, F32), jnp.asarray(sin, F32)


def _deinterleave_matrix():
    p = np.zeros((2 * LANES, 2 * LANES), np.float32)
    i = np.arange(LANES)
    p[2 * i, i] = 1.0
    p[2 * i + 1, LANES + i] = 1.0
    return jnp.asarray(p, BF16)


def kernel(x, c, ctx, c_ctx, w_mod, b_mod, g_norm1, g_norm2, w_in, g_qnorm, g_knorm, lambda_q1, lambda_k1,
           lambda_q2, lambda_k2, g_subln, w_pool, pool_scale, w_out, w_router, b_router, w_expert_in,
           b_expert_in, w_expert_out, b_expert_out, g_final):
    b, seq, d = x.shape
    ctx_len = ctx.shape[1]
    depth = w_mod.shape[0]
    assert ctx_len == TM and seq % TM == 0

    xa = jnp.concatenate([ctx, x], axis=1)
    cc = jnp.concatenate([c, c_ctx[None, :], jnp.zeros((8 - b - 1, d), F32)], axis=0)
    mods = _modulation(cc, w_mod, b_mod)
    rope_a = _rope_tables(ctx_len, seq, HEAD_DIM, HEAD_DIM)
    rope_b = _rope_tables(ctx_len, seq, B_QK_DIM, B_QK_DIM)
    e, _, f2 = w_expert_in.shape[1:]
    w1p_all = _deinterleave_weights(w_expert_in.reshape(depth * e, d, f2), _deinterleave_matrix())
    b1p_all = jnp.concatenate([b_expert_in[..., 0::2], b_expert_in[..., 1::2]], axis=-1).reshape(depth * e, 1, f2)
    w2_all = w_expert_out.reshape(depth * e, f2 // 2, d)
    b2_all = b_expert_out.reshape(depth * e, 1, d)

    for l in range(depth):
        last = l == depth - 1
        q0 = 1 if last else 0
        lam_init = 0.8 - 0.6 * math.exp(-0.3 * l)
        mod3 = mods[l].reshape(8, 1, 6 * d)
        qa, ka, vat, qb, kb, vbt, uc = _in_projection(
            xa, mod3, g_norm1[l][None], w_in[l].astype(BF16), g_qnorm[l][None], g_knorm[l][None],
            rope_a + rope_b)
        oa = _attention_a(qa, ka, vat, q0)
        lam_vecs = jnp.stack([lambda_q1[l], lambda_k1[l], lambda_q2[l], lambda_k2[l]])
        ob = _attention_b(qb, kb, vbt, lam_vecs, g_subln[l][:, None], q0, lam_init)
        oc = _pooling(uc, w_pool[l].astype(BF16), pool_scale[l][None], q0, ctx_len)

        wr = jnp.pad(w_router[l], ((0, 0), (0, LANES - e)))
        wr_hi = wr.astype(BF16)
        wr2 = jnp.concatenate([wr_hi, (wr - wr_hi.astype(F32)).astype(BF16)], axis=1)
        br = jnp.pad(b_router[l], (0, LANES - e))[None]
        xmid, f_moe, top_e, rank, gate_t, counts = _post_attention(
            oa, ob, oc, xa, mod3, g_norm2[l][None], w_out[l].astype(BF16), wr2, br, q0)

        ntok = top_e.shape[1]
        n_tiles = (TOP_K * ntok) // E_TM + e
        pos_flat, tile_expert, tile_block, tile_rows, tile_end = _routing_tables(top_e, rank, counts, n_tiles)
        xs = _dispatch(pos_flat, tile_end, f_moe, n_tiles * E_TM)
        ys = _experts(tile_expert + l * e, tile_block, tile_rows, xs, w1p_all, b1p_all, w2_all, b2_all)
        xa = _combine(pos_flat, ys, gate_t, xmid, mod3, g_final[None], q0, last)
    return xa
```

```python
import functools
import math

import numpy as np
import jax
import jax.numpy as jnp
from jax import lax
from jax.experimental import pallas as pl
from jax.experimental.pallas import tpu as pltpu

F32 = jnp.float32
BF16 = jnp.bfloat16
U32 = jnp.uint32
I32 = jnp.int32

HEAD_DIM = 128
A_HEADS = 8
A_KV_HEADS = 2
GQA_GROUP = A_HEADS // A_KV_HEADS
B_HEADS = 4
B_QK_DIM = 64
POOL_WINDOWS = (2, 4, 8, 16)
POOL_GROUP = 128
C_WIDTH = len(POOL_WINDOWS) * POOL_GROUP
A_Q = A_HEADS * HEAD_DIM
A_KV = A_KV_HEADS * HEAD_DIM
B_QK = B_HEADS * 2 * B_QK_DIM
B_V = B_HEADS * 2 * B_QK_DIM
N_EXPERTS = 32
TOP_K = 4
SWIGLU_LIMIT = 7.0
SWIGLU_ALPHA = 1.702
ROPE_THETA = 10000.0
GRID_W = 64
EPS = 1e-6

TM = 256
KV_TILE = 2048
BF16_ROWS = 16
VT_ROWS = 128 + BF16_ROWS
A_STREAM_HEADS = 2
E_TM = 512
POOL_HALO = 8
LANES = 128
SUBLANES = 8
VMEM_LIMIT = 56 * 1024 * 1024
LOG2E = math.log2(math.e)
NEG_BIG = -1e30


def _cparams(sem, vmem=None):
    return pltpu.CompilerParams(dimension_semantics=sem, vmem_limit_bytes=vmem)


def _sigmoid(z):
    return 1.0 / (1.0 + jnp.exp(-z))


def _mod_kernel(cc_ref, w_ref, b_ref, o_ref):
    cc = cc_ref[...]
    a = (cc * _sigmoid(cc)).astype(BF16)
    o_ref[0] = jnp.dot(a, w_ref[0].astype(BF16), preferred_element_type=F32) + b_ref[0]


def _modulation(cc, w_mod, b_mod):
    depth, d, n = w_mod.shape
    tn = 1536
    return pl.pallas_call(
        _mod_kernel,
        grid=(depth, n // tn),
        in_specs=[
            pl.BlockSpec((8, d), lambda l, j: (0, 0)),
            pl.BlockSpec((1, d, tn), lambda l, j: (l, 0, j)),
            pl.BlockSpec((1, 1, tn), lambda l, j: (l, 0, j)),
        ],
        out_specs=pl.BlockSpec((1, 8, tn), lambda l, j: (l, 0, j)),
        out_shape=jax.ShapeDtypeStruct((depth, 8, n), F32),
        compiler_params=_cparams(("parallel", "parallel"), VMEM_LIMIT),
        name="modulation",
    )(cc, w_mod, b_mod.reshape(depth, 1, n))


def _rope(y, cos, sin_signed, half):
    width = y.shape[-1]
    lane = lax.broadcasted_iota(I32, y.shape, 1)
    partner = jnp.where(lane % (2 * half) < half,
                        pltpu.roll(y, width - half, 1), pltpu.roll(y, half, 1))
    return y * cos + partner * sin_signed


def _head_rms(y, g):
    return y * lax.rsqrt(jnp.mean(y * y, axis=-1, keepdims=True) + EPS) * g


def _proj_kernel(x_ref, mod_ref, g1_ref, w_ref, gq_ref, gk_ref, ca_ref, sa_ref, cb_ref, sb_ref,
                 qa_ref, ka_ref, vat_ref, qb_ref, kb_ref, vbt_ref, uc_ref):
    d = x_ref.shape[-1]
    x = x_ref[0]
    xn = x * lax.rsqrt(jnp.mean(x * x, axis=-1, keepdims=True) + EPS) * g1_ref[...]
    sh = mod_ref[0, :, 0:d]
    sc = mod_ref[0, :, d:2 * d]
    h = (xn * (1.0 + sc) + sh).astype(BF16)
    ca, sa, cb, sb = ca_ref[...], sa_ref[...], cb_ref[...], sb_ref[...]

    qa_scale = LOG2E / math.sqrt(HEAD_DIM)
    qb_scale = LOG2E / math.sqrt(B_QK_DIM)
    ones = jnp.ones((BF16_ROWS, x.shape[0]), BF16)

    def store_vt(ref, hd, v):
        ref[0, hd * VT_ROWS:hd * VT_ROWS + HEAD_DIM, :] = v.T.astype(BF16)
        ref[0, hd * VT_ROWS + HEAD_DIM:(hd + 1) * VT_ROWS, :] = ones

    c0 = 0
    pq = jnp.dot(h, w_ref[:, c0:c0 + A_Q], preferred_element_type=F32)
    for hd in range(A_HEADS):
        y = _head_rms(pq[:, hd * HEAD_DIM:(hd + 1) * HEAD_DIM], gq_ref[...])
        qa_ref[0, :, hd * HEAD_DIM:(hd + 1) * HEAD_DIM] = (
            _rope(y, ca, sa, HEAD_DIM // 4) * qa_scale).astype(BF16)
    c0 += A_Q
    pkv = jnp.dot(h, w_ref[:, c0:c0 + 2 * A_KV], preferred_element_type=F32)
    for hd in range(A_KV_HEADS):
        y = _head_rms(pkv[:, hd * HEAD_DIM:(hd + 1) * HEAD_DIM], gk_ref[...])
        ka_ref[0, :, hd * HEAD_DIM:(hd + 1) * HEAD_DIM] = _rope(y, ca, sa, HEAD_DIM // 4).astype(BF16)
        store_vt(vat_ref, hd, pkv[:, A_KV + hd * HEAD_DIM:A_KV + (hd + 1) * HEAD_DIM])
    c0 += 2 * A_KV
    pb = jnp.dot(h, w_ref[:, c0:c0 + 2 * B_QK], preferred_element_type=F32)
    lane = lax.broadcasted_iota(I32, (x.shape[0], LANES), 1)

    def head_pair(base, hd):
        v0 = pb[:, base + (hd // 2) * LANES:base + (hd // 2 + 1) * LANES]
        v1 = pb[:, base + B_QK // 2 + (hd // 2) * LANES:base + B_QK // 2 + (hd // 2 + 1) * LANES]
        if hd % 2 == 0:
            return jnp.where(lane < B_QK_DIM, v0, pltpu.roll(v1, B_QK_DIM, 1))
        return jnp.where(lane < B_QK_DIM, pltpu.roll(v0, B_QK_DIM, 1), v1)

    for hd in range(B_HEADS):
        sl = slice(hd * LANES, (hd + 1) * LANES)
        qb_ref[0, :, sl] = (_rope(head_pair(0, hd), cb, sb, B_QK_DIM // 4) * qb_scale).astype(BF16)
        kb_ref[0, :, sl] = _rope(head_pair(B_QK, hd), cb, sb, B_QK_DIM // 4).astype(BF16)
    c0 += 2 * B_QK
    pvu = jnp.dot(h, w_ref[:, c0:c0 + B_V + C_WIDTH], preferred_element_type=F32)
    for hd in range(B_HEADS):
        store_vt(vbt_ref, hd, pvu[:, hd * HEAD_DIM:(hd + 1) * HEAD_DIM])
    uc_ref[0] = pvu[:, B_V:B_V + C_WIDTH]


def _in_projection(xa, mod3, g1, w_in_p, gq, gk, rope):
    b, t, d = xa.shape
    nt = t // TM
    n_in = w_in_p.shape[1]
    tok = lambda w: pl.BlockSpec((1, TM, w), lambda bb, i: (bb, i, 0))
    tok_t = lambda w: pl.BlockSpec((1, w, TM), lambda bb, i: (bb, 0, i))
    const2 = lambda r, c: pl.BlockSpec((r, c), lambda bb, i: (0, 0))
    tab = pl.BlockSpec((TM, LANES), lambda bb, i: (i, 0))
    return pl.pallas_call(
        _proj_kernel,
        grid=(b, nt),
        in_specs=[
            tok(d),
            pl.BlockSpec((1, 1, mod3.shape[-1]), lambda bb, i: (jnp.where(i == 0, b, bb), 0, 0)),
            const2(1, d), const2(d, n_in), const2(1, HEAD_DIM), const2(1, HEAD_DIM),
            tab, tab, tab, tab,
        ],
        out_specs=[tok(A_Q), tok(A_KV), tok_t(A_KV_HEADS * VT_ROWS), tok(B_QK), tok(B_QK),
                   tok_t(B_HEADS * VT_ROWS), tok(C_WIDTH)],
        out_shape=[
            jax.ShapeDtypeStruct((b, t, A_Q), BF16),
            jax.ShapeDtypeStruct((b, t, A_KV), BF16),
            jax.ShapeDtypeStruct((b, A_KV_HEADS * VT_ROWS, t), BF16),
            jax.ShapeDtypeStruct((b, t, B_QK), BF16),
            jax.ShapeDtypeStruct((b, t, B_QK), BF16),
            jax.ShapeDtypeStruct((b, B_HEADS * VT_ROWS, t), BF16),
            jax.ShapeDtypeStruct((b, t, C_WIDTH), F32),
        ],
        compiler_params=_cparams(("parallel", "parallel"), VMEM_LIMIT),
        name="in_projection",
    )(xa, mod3, g1, w_in_p, gq, gk, *rope)


def _flash_t(q_stacks, kv_of, k_ref, vt_ref, m_ref, acc_ref, n_kv):
    def step(off, size):
        ss = [lax.dot_general(k_ref[0, pl.ds(off, size), kv * LANES:(kv + 1) * LANES], q,
                              (((1,), (1,)), ((), ())), preferred_element_type=F32)
              for kv, q in zip(kv_of, q_stacks)]
        for h, (kv, s) in enumerate(zip(kv_of, ss)):
            m = m_ref[h]
            m_new = jnp.maximum(m, jnp.max(s, axis=0, keepdims=True))
            p = jnp.exp2(s - m_new)
            vt = vt_ref[0, kv * VT_ROWS:(kv + 1) * VT_ROWS, pl.ds(off, size)]
            acc_ref[h] = (jnp.exp2(m - m_new) * acc_ref[h]
                          + jnp.dot(vt, p.astype(BF16), preferred_element_type=F32))
            m_ref[h] = m_new

    m_ref[...] = jnp.full(m_ref.shape, NEG_BIG, F32)
    acc_ref[...] = jnp.zeros(acc_ref.shape, F32)
    step(0, TM)

    def body(j, carry):
        step(pl.multiple_of(TM + j * KV_TILE, TM), KV_TILE)
        return carry

    lax.fori_loop(0, n_kv, body, 0)
    return [acc_ref[h, 0:HEAD_DIM] / acc_ref[h, HEAD_DIM:HEAD_DIM + 1] for h in range(len(q_stacks))]


def _attn_a_kernel(q_ref, k_ref, vt_ref, o_ref, m_ref, acc_ref, *, q0, n_kv_full):
    qi = pl.program_id(1) + q0
    n_kv = jnp.where(qi == 0, 0, n_kv_full)
    q = q_ref[0]
    n_streams = A_HEADS // A_STREAM_HEADS
    q_stacks = [jnp.concatenate([q[:, (s * A_STREAM_HEADS + g) * HEAD_DIM:(s * A_STREAM_HEADS + g + 1) * HEAD_DIM]
                                 for g in range(A_STREAM_HEADS)], axis=0) for s in range(n_streams)]
    kv_of = [s * A_STREAM_HEADS // GQA_GROUP for s in range(n_streams)]
    for s, o in enumerate(_flash_t(q_stacks, kv_of, k_ref, vt_ref, m_ref, acc_ref, n_kv)):
        for g in range(A_STREAM_HEADS):
            c0 = (s * A_STREAM_HEADS + g) * HEAD_DIM
            o_ref[0, :, c0:c0 + HEAD_DIM] = o[:, g * TM:(g + 1) * TM].T.astype(BF16)


def _attn_b_kernel(q_ref, k_ref, vt_ref, lam_ref, gs_ref, o_ref, m_ref, acc_ref, *, q0, n_kv_full, lam_init):
    qi = pl.program_id(1) + q0
    n_kv = jnp.where(qi == 0, 0, n_kv_full)
    lane = lax.broadcasted_iota(I32, (TM, LANES), 1)
    zero = jnp.zeros((TM, LANES), BF16)
    q_stacks = []
    for h in range(B_HEADS):
        q = q_ref[0, :, h * LANES:(h + 1) * LANES]
        q_stacks.append(jnp.concatenate(
            [jnp.where(lane < B_QK_DIM, q, zero), jnp.where(lane >= B_QK_DIM, q, zero)], axis=0))
    res = _flash_t(q_stacks, list(range(B_HEADS)), k_ref, vt_ref, m_ref, acc_ref, n_kv)
    lv = lam_ref[...]
    lam = (jnp.exp(jnp.sum(lv[0:1] * lv[1:2], axis=-1, keepdims=True))
           - jnp.exp(jnp.sum(lv[2:3] * lv[3:4], axis=-1, keepdims=True)) + lam_init)
    for h, o in enumerate(res):
        od = o[:, 0:TM] - lam * o[:, TM:2 * TM]
        r = lax.rsqrt(jnp.mean(od * od, axis=0, keepdims=True) + EPS)
        y = od * r * gs_ref[...] * (1.0 - lam_init)
        o_ref[0, :, h * LANES:(h + 1) * LANES] = y.T.astype(BF16)


def _attention_a(qa, ka, vat, q0):
    b, t, _ = qa.shape
    nt = t // TM
    return pl.pallas_call(
        functools.partial(_attn_a_kernel, q0=q0, n_kv_full=(t - TM) // KV_TILE),
        grid=(b, nt - q0),
        in_specs=[
            pl.BlockSpec((1, TM, A_Q), lambda bb, i: (bb, i + q0, 0)),
            pl.BlockSpec((1, t, A_KV), lambda bb, i: (bb, 0, 0)),
            pl.BlockSpec((1, A_KV_HEADS * VT_ROWS, t), lambda bb, i: (bb, 0, 0)),
        ],
        out_specs=pl.BlockSpec((1, TM, A_Q), lambda bb, i: (bb, i, 0)),
        out_shape=jax.ShapeDtypeStruct((b, t - q0 * TM, A_Q), BF16),
        scratch_shapes=[pltpu.VMEM((A_HEADS // A_STREAM_HEADS, 1, A_STREAM_HEADS * TM), F32),
                        pltpu.VMEM((A_HEADS // A_STREAM_HEADS, VT_ROWS, A_STREAM_HEADS * TM), F32)],
        compiler_params=_cparams(("parallel", "parallel"), VMEM_LIMIT),
        name="attention_gqa",
    )(qa, ka, vat)


def _attention_b(qb, kb, vbt, lam_vecs, g_sub_col, q0, lam_init):
    b, t, _ = qb.shape
    nt = t // TM
    return pl.pallas_call(
        functools.partial(_attn_b_kernel, q0=q0, n_kv_full=(t - TM) // KV_TILE, lam_init=lam_init),
        grid=(b, nt - q0),
        in_specs=[
            pl.BlockSpec((1, TM, B_QK), lambda bb, i: (bb, i + q0, 0)),
            pl.BlockSpec((1, t, B_QK), lambda bb, i: (bb, 0, 0)),
            pl.BlockSpec((1, B_HEADS * VT_ROWS, t), lambda bb, i: (bb, 0, 0)),
            pl.BlockSpec((4, B_QK_DIM), lambda bb, i: (0, 0)),
            pl.BlockSpec((LANES, 1), lambda bb, i: (0, 0)),
        ],
        out_specs=pl.BlockSpec((1, TM, B_V), lambda bb, i: (bb, i, 0)),
        out_shape=jax.ShapeDtypeStruct((b, t - q0 * TM, B_V), BF16),
        scratch_shapes=[pltpu.VMEM((B_HEADS, 1, 2 * TM), F32), pltpu.VMEM((B_HEADS, VT_ROWS, 2 * TM), F32)],
        compiler_params=_cparams(("parallel", "parallel"), VMEM_LIMIT),
        name="attention_diff",
    )(qb, kb, vbt, lam_vecs, g_sub_col)


def _pool_kernel(prev_ref, cur_ref, next_ref, wp_ref, ps_ref, o_ref, *, q0, ctx_len, total_len):
    i = pl.program_id(1) + q0
    ext = jnp.concatenate([prev_ref[0], cur_ref[0], next_ref[0]], axis=0)
    rows = ext.shape[0]
    seg_lo = jnp.where(i == 0, 0, ctx_len)
    seg_hi = jnp.where(i == 0, ctx_len, total_len)
    grow = i * TM - POOL_HALO + lax.broadcasted_iota(I32, (rows, 1), 0)
    ext = jnp.where((grow >= seg_lo) & (grow < seg_hi), ext, 0.0)
    tpos = grow[POOL_HALO:POOL_HALO + TM] - seg_lo
    seg_len = seg_hi - seg_lo

    def back(a, d):
        return pltpu.roll(a, d, 0)

    def fwd(a, d):
        return pltpu.roll(a, rows - d, 0)

    for gi, w in enumerate(POOL_WINDOWS):
        u = ext[:, gi * POOL_GROUP:(gi + 1) * POOL_GROUP]
        acc = u + back(u, 1)
        span = 2
        while span < w:
            acc = acc + back(acc, span)
            span *= 2
        if w > 2:
            acc = fwd(acc, w // 2 - 1)
        cnt = (jnp.minimum(tpos + w // 2, seg_len) - jnp.maximum(tpos - w // 2, 0)).astype(F32)
        sl = slice(POOL_HALO, POOL_HALO + TM)
        p = (acc[sl] / cnt - u[sl]).astype(BF16)
        y = jnp.dot(p, wp_ref[gi], preferred_element_type=F32)
        o_ref[0, :, gi * POOL_GROUP:(gi + 1) * POOL_GROUP] = (
            y * ps_ref[:, gi * POOL_GROUP:(gi + 1) * POOL_GROUP]).astype(BF16)


def _pooling(uc, w_pool_b, pool_scale, q0, ctx_len):
    b, t, cw = uc.shape
    nt = t // TM
    per = TM // POOL_HALO
    last = t // POOL_HALO - 1
    return pl.pallas_call(
        functools.partial(_pool_kernel, q0=q0, ctx_len=ctx_len, total_len=t),
        grid=(b, nt - q0),
        in_specs=[
            pl.BlockSpec((1, POOL_HALO, cw), lambda bb, i: (bb, jnp.maximum((i + q0) * per - 1, 0), 0)),
            pl.BlockSpec((1, TM, cw), lambda bb, i: (bb, i + q0, 0)),
            pl.BlockSpec((1, POOL_HALO, cw), lambda bb, i: (bb, jnp.minimum((i + q0 + 1) * per, last), 0)),
            pl.BlockSpec(w_pool_b.shape, lambda bb, i: (0, 0, 0)),
            pl.BlockSpec((1, cw), lambda bb, i: (0, 0)),
        ],
        out_specs=pl.BlockSpec((1, TM, cw), lambda bb, i: (bb, i, 0)),
        out_shape=jax.ShapeDtypeStruct((b, t - q0 * TM, cw), BF16),
        compiler_params=_cparams(("parallel", "parallel")),
        name="pooling",
    )(uc, uc, uc, w_pool_b, pool_scale)


def _post_kernel(oa_ref, ob_ref, oc_ref, x_ref, mod_ref, g2_ref, wo_ref, wr_ref, br_ref,
                 xmid_ref, f_ref, te_ref, rk_ref, gt_ref, cnt_ref, carry_ref):
    d = x_ref.shape[-1]
    first = (pl.program_id(0) == 0) & (pl.program_id(1) == 0)

    @pl.when(first)
    def _():
        carry_ref[...] = jnp.zeros_like(carry_ref)

    m = (jnp.dot(oa_ref[0], wo_ref[0:A_Q], preferred_element_type=F32)
         + jnp.dot(ob_ref[0], wo_ref[A_Q:A_Q + B_V], preferred_element_type=F32)
         + jnp.dot(oc_ref[0], wo_ref[A_Q + B_V:A_Q + B_V + C_WIDTH], preferred_element_type=F32))
    x = x_ref[0] + mod_ref[0, :, 2 * d:3 * d] * m
    xmid_ref[0] = x
    xn = x * lax.rsqrt(jnp.mean(x * x, axis=-1, keepdims=True) + EPS) * g2_ref[...]
    f = xn * (1.0 + mod_ref[0, :, 4 * d:5 * d]) + mod_ref[0, :, 3 * d:4 * d]
    f_ref[0] = f

    f_hi = f.astype(BF16)
    f_lo = (f - f_hi.astype(F32)).astype(BF16)
    lg2 = jnp.dot(f_hi, wr_ref[...], preferred_element_type=F32)
    lg1 = jnp.dot(f_lo, wr_ref[:, 0:LANES], preferred_element_type=F32)
    logits = lg2[:, 0:LANES] + lg2[:, LANES:2 * LANES] + lg1 + br_ref[...]
    work = logits.T[0:N_EXPERTS]

    e_id = lax.broadcasted_iota(I32, work.shape, 0).astype(F32)
    vals, idxs, hots = [], [], []
    for _ in range(TOP_K):
        mx = jnp.max(work, axis=0, keepdims=True)
        idx = jnp.min(jnp.where(work == mx, e_id, float(N_EXPERTS)), axis=0, keepdims=True)
        hot = e_id == idx
        vals.append(mx)
        idxs.append(idx)
        hots.append(hot)
        work = jnp.where(hot, -jnp.inf, work)
    ex = [jnp.exp(v - vals[0]) for v in vals]
    den = ex[0] + ex[1] + ex[2] + ex[3]
    gates = [e / den for e in ex]

    msel = jnp.zeros(work.shape, F32)
    for hot in hots:
        msel = msel + jnp.where(hot, 1.0, 0.0)
    r_i = lax.broadcasted_iota(I32, (TM, TM), 0)
    c_i = lax.broadcasted_iota(I32, (TM, TM), 1)
    tri = jnp.where(r_i < c_i, 1.0, 0.0).astype(BF16)
    carry = carry_ref[:, 0:1]
    rank_full = jnp.dot(msel.astype(BF16), tri, preferred_element_type=F32) + carry
    ranks = [jnp.sum(jnp.where(hot, rank_full, 0.0), axis=0, keepdims=True) for hot in hots]
    new_carry = carry + jnp.sum(msel, axis=1, keepdims=True)
    carry_ref[...] = jnp.broadcast_to(new_carry, carry_ref.shape)
    cnt_ref[...] = jnp.broadcast_to(new_carry, cnt_ref.shape)

    row8 = lax.broadcasted_iota(I32, (8, TM), 0)

    def rows8(vs):
        out = jnp.zeros((8, TM), F32)
        for k, v in enumerate(vs):
            out = jnp.where(row8 == k, v, out)
        return out

    te_ref[...] = rows8(idxs).astype(I32)
    rk_ref[...] = rows8(ranks).astype(I32)
    g128 = jnp.concatenate([rows8(gates), jnp.zeros((LANES - 8, TM), F32)], axis=0)
    gt_ref[...] = g128.T


def _post_attention(oa, ob, oc, xa, mod3, g2, w_out_b, wr2, br, q0):
    b, t, d = xa.shape
    nt = t // TM - q0
    ntok = b * nt * TM
    tok = lambda w: pl.BlockSpec((1, TM, w), lambda bb, i: (bb, i, 0))
    const2 = lambda r, c: pl.BlockSpec((r, c), lambda bb, i: (0, 0))
    flat = lambda r: pl.BlockSpec((r, TM), lambda bb, i: (0, bb * nt + i))
    return pl.pallas_call(
        _post_kernel,
        grid=(b, nt),
        in_specs=[
            tok(A_Q), tok(B_V), tok(C_WIDTH),
            pl.BlockSpec((1, TM, d), lambda bb, i: (bb, i + q0, 0)),
            pl.BlockSpec((1, 1, mod3.shape[-1]), lambda bb, i: (jnp.where(i + q0 == 0, b, bb), 0, 0)),
            const2(1, d), const2(d, d), const2(d, 2 * LANES), const2(1, LANES),
        ],
        out_specs=[
            tok(d), tok(d), flat(8), flat(8),
            pl.BlockSpec((TM, LANES), lambda bb, i: (bb * nt + i, 0)),
            const2(N_EXPERTS, LANES),
        ],
        out_shape=[
            jax.ShapeDtypeStruct((b, nt * TM, d), F32),
            jax.ShapeDtypeStruct((b, nt * TM, d), F32),
            jax.ShapeDtypeStruct((8, ntok), I32),
            jax.ShapeDtypeStruct((8, ntok), I32),
            jax.ShapeDtypeStruct((ntok, LANES), F32),
            jax.ShapeDtypeStruct((N_EXPERTS, LANES), F32),
        ],
        scratch_shapes=[pltpu.VMEM((N_EXPERTS, LANES), F32)],
        compiler_params=_cparams(("arbitrary", "arbitrary"), VMEM_LIMIT),
        name="out_projection_router",
    )(oa, ob, oc, xa, mod3, g2, w_out_b, wr2, br)


def _row_copies(pos_ref, src_row, dst_row, sem):
    def issue(i, carry):
        base = pl.multiple_of(i * SUBLANES, SUBLANES)
        for j in range(SUBLANES):
            for k in range(TOP_K):
                p = pos_ref[k * TM + base + j]
                pltpu.make_async_copy(src_row(k, base + j, p), dst_row(k, base + j, p), sem).start(priority=k % 2)
        return carry

    lax.fori_loop(0, TM // SUBLANES, issue, 0)


def _dispatch_kernel(pos_ref, tend_ref, f_ref, xs_ref, zbuf, sem, zsem):
    first = (pl.program_id(0) == 0) & (pl.program_id(1) == 0)

    @pl.when(first)
    def _():
        zbuf[...] = jnp.zeros_like(zbuf)

        def last_tile_copy(e):
            return pltpu.make_async_copy(zbuf, xs_ref.at[pl.ds((tend_ref[e] - 1) * E_TM, E_TM)], zsem)

        def has_tiles(e):
            return tend_ref[e] > (tend_ref[e - 1] if e else 0)

        def tail_copy(i):
            return pltpu.make_async_copy(zbuf, xs_ref.at[pl.ds(i * E_TM, E_TM)], zsem)

        n_used, n_tiles = tend_ref[N_EXPERTS - 1], xs_ref.shape[0] // E_TM
        for e in range(N_EXPERTS):
            pl.when(has_tiles(e))(lambda e=e: last_tile_copy(e).start())
        lax.fori_loop(n_used, n_tiles, lambda i, c: (tail_copy(i).start(), c)[1], 0)
        for e in range(N_EXPERTS):
            pl.when(has_tiles(e))(lambda e=e: last_tile_copy(e).wait())
        lax.fori_loop(n_used, n_tiles, lambda i, c: (tail_copy(i).wait(), c)[1], 0)

    _row_copies(pos_ref,
                lambda k, t, p: f_ref.at[0, pl.ds(t, 1)],
                lambda k, t, p: xs_ref.at[pl.ds(p, 1)], sem)
    for k in range(TOP_K):
        pltpu.make_async_copy(f_ref.at[0], xs_ref.at[pl.ds(0, TM)], sem).wait()


def _dispatch(pos_flat, tile_end, f, n_rows):
    b, t, hw = f.shape
    nt = t // TM
    return pl.pallas_call(
        _dispatch_kernel,
        grid=(b, nt),
        in_specs=[
            pl.BlockSpec((TOP_K * TM,), lambda bb, i: (bb * nt + i,), memory_space=pltpu.SMEM),
            pl.BlockSpec(memory_space=pltpu.SMEM),
            pl.BlockSpec((1, TM, hw), lambda bb, i: (bb, i, 0)),
        ],
        out_specs=pl.BlockSpec(memory_space=pl.ANY),
        out_shape=jax.ShapeDtypeStruct((n_rows, hw), F32),
        scratch_shapes=[pltpu.VMEM((E_TM, hw), F32), pltpu.SemaphoreType.DMA(()), pltpu.SemaphoreType.DMA(())],
        compiler_params=_cparams(("arbitrary", "arbitrary")),
        name="moe_dispatch",
    )(pos_flat, tile_end, f)


def _deinterleave_kernel(w_ref, p_ref, o_ref):
    fdim = w_ref.shape[2] // 2
    for j in range(fdim // LANES):
        chunk = w_ref[0, :, 2 * LANES * j:2 * LANES * (j + 1)].astype(BF16)
        r = jnp.dot(chunk, p_ref[...], preferred_element_type=F32)
        o_ref[0, :, LANES * j:LANES * (j + 1)] = r[:, 0:LANES].astype(BF16)
        o_ref[0, :, fdim + LANES * j:fdim + LANES * (j + 1)] = r[:, LANES:2 * LANES].astype(BF16)


def _deinterleave_weights(w, perm):
    n, d, f2 = w.shape
    tk = 512
    return pl.pallas_call(
        _deinterleave_kernel,
        grid=(n, d // tk),
        in_specs=[
            pl.BlockSpec((1, tk, f2), lambda e, j: (e, j, 0)),
            pl.BlockSpec(perm.shape, lambda e, j: (0, 0)),
        ],
        out_specs=pl.BlockSpec((1, tk, f2), lambda e, j: (e, j, 0)),
        out_shape=jax.ShapeDtypeStruct((n, d, f2), BF16),
        compiler_params=_cparams(("parallel", "parallel")),
        name="expert_weight_layout",
    )(w, perm)


def _experts_kernel(te_ref, tb_ref, rows_ref, x_ref, w1_ref, b1_ref, w2_ref, b2_ref, y_ref, w2b_ref):
    del tb_ref
    fdim = w2_ref.shape[1]
    i = pl.program_id(0)
    used = rows_ref[i] > 0

    @pl.when(jnp.logical_not(used))
    def _():
        y_ref[...] = jnp.zeros_like(y_ref)

    @pl.when(used & ((i == 0) | (te_ref[i] != te_ref[jnp.maximum(i - 1, 0)])))
    def _():
        w2b_ref[...] = w2_ref[0].astype(BF16)

    @pl.when(used)
    def _():
        u = jnp.dot(x_ref[...].astype(BF16), w1_ref[0], preferred_element_type=F32) + b1_ref[0]
        glu = jnp.minimum(u[:, 0:fdim], SWIGLU_LIMIT)
        lin = jnp.clip(u[:, fdim:2 * fdim], -SWIGLU_LIMIT, SWIGLU_LIMIT)
        a = glu * _sigmoid(SWIGLU_ALPHA * glu) * (lin + 1.0)
        y_ref[...] = jnp.dot(a.astype(BF16), w2b_ref[...], preferred_element_type=F32) + b2_ref[0]


def _experts(tile_expert, tile_block, tile_rows, xs, w1p, b1p, w2, b2):
    n_rows, hw = xs.shape
    n_tiles = n_rows // E_TM
    _, d, f2 = w1p.shape
    fdim = f2 // 2
    grid_spec = pltpu.PrefetchScalarGridSpec(
        num_scalar_prefetch=3,
        grid=(n_tiles,),
        in_specs=[
            pl.BlockSpec((E_TM, hw), lambda i, te, tb, nu: (tb[i], 0)),
            pl.BlockSpec((1, d, f2), lambda i, te, tb, nu: (te[i], 0, 0)),
            pl.BlockSpec((1, 1, f2), lambda i, te, tb, nu: (te[i], 0, 0)),
            pl.BlockSpec((1, fdim, d), lambda i, te, tb, nu: (te[i], 0, 0)),
            pl.BlockSpec((1, 1, d), lambda i, te, tb, nu: (te[i], 0, 0)),
        ],
        out_specs=pl.BlockSpec((E_TM, hw), lambda i, te, tb, nu: (i, 0)),
        scratch_shapes=[pltpu.VMEM((fdim, d), BF16)],
    )
    return pl.pallas_call(
        _experts_kernel,
        grid_spec=grid_spec,
        out_shape=jax.ShapeDtypeStruct((n_rows, hw), F32),
        compiler_params=_cparams(("arbitrary",), 60 * 1024 * 1024),
        name="moe_experts",
    )(tile_expert, tile_block, tile_rows, xs, w1p, b1p, w2, b2)


def _combine_kernel(pos_ref, ys_ref, gt_ref, x_ref, mod_ref, gf_ref, o_ref, ybuf, sem, *, final):
    d = x_ref.shape[-1]
    _row_copies(pos_ref,
                lambda k, t, p: ys_ref.at[pl.ds(p, 1)],
                lambda k, t, p: ybuf.at[k, pl.ds(t, 1)], sem)
    for k in range(TOP_K):
        pltpu.make_async_copy(ys_ref.at[pl.ds(0, TM)], ybuf.at[k], sem).wait()
    acc = gt_ref[:, 0:1] * ybuf[0]
    for k in range(1, TOP_K):
        acc = acc + gt_ref[:, k:k + 1] * ybuf[k]
    xo = x_ref[0] + mod_ref[0, :, 5 * d:6 * d] * acc
    if final:
        xo = xo * lax.rsqrt(jnp.mean(xo * xo, axis=-1, keepdims=True) + EPS) * gf_ref[...]
    o_ref[0] = xo


def _combine(pos_flat, ys, gate_t, xmid, mod3, g_final, q0, final):
    b, t, d = xmid.shape
    nt = t // TM
    return pl.pallas_call(
        functools.partial(_combine_kernel, final=final),
        grid=(b, nt),
        in_specs=[
            pl.BlockSpec((TOP_K * TM,), lambda bb, i: (bb * nt + i,), memory_space=pltpu.SMEM),
            pl.BlockSpec(memory_space=pl.ANY),
            pl.BlockSpec((TM, LANES), lambda bb, i: (bb * nt + i, 0)),
            pl.BlockSpec((1, TM, d), lambda bb, i: (bb, i, 0)),
            pl.BlockSpec((1, 1, mod3.shape[-1]), lambda bb, i: (jnp.where(i + q0 == 0, b, bb), 0, 0)),
            pl.BlockSpec((1, d), lambda bb, i: (0, 0)),
        ],
        out_specs=pl.BlockSpec((1, TM, d), lambda bb, i: (bb, i, 0)),
        out_shape=jax.ShapeDtypeStruct((b, t, d), F32),
        scratch_shapes=[pltpu.VMEM((TOP_K, TM, d), F32), pltpu.SemaphoreType.DMA(())],
        compiler_params=_cparams(("arbitrary", "arbitrary"), VMEM_LIMIT),
        name="moe_combine",
    )(pos_flat, ys, gate_t, xmid, mod3, g_final)


def _routing_tables(top_e, rank, counts, n_tiles):
    cnt = counts[:, 0].astype(I32)
    tiles_e = (cnt + E_TM - 1) // E_TM
    tile_end = jnp.cumsum(tiles_e)
    row_start = (tile_end - tiles_e) * E_TM
    hit = top_e[0:TOP_K, :, None] == jnp.arange(N_EXPERTS, dtype=I32)
    pos = jnp.sum(jnp.where(hit, row_start, 0), axis=-1) + rank[0:TOP_K]
    ntok = pos.shape[1]
    pos_flat = pos.reshape(TOP_K, ntok // TM, TM).transpose(1, 0, 2).reshape(-1)
    n_used = tile_end[-1]
    tile_id = jnp.arange(n_tiles, dtype=I32)
    ti = jnp.minimum(tile_id, n_used - 1)
    tile_expert = jnp.minimum(jnp.sum((tile_end[None, :] <= ti[:, None]).astype(I32), axis=1), N_EXPERTS - 1)
    mine = tile_expert[:, None] == jnp.arange(N_EXPERTS, dtype=I32)
    rows_left = jnp.sum(jnp.where(mine, cnt + row_start, 0), axis=1) - ti * E_TM
    tile_rows = jnp.where(tile_id < n_used, jnp.clip(rows_left, 0, E_TM), 0).astype(I32)
    return pos_flat, tile_expert, ti, tile_rows, tile_end.astype(I32)


def _rope_tables(ctx_len, seq, head_dim, lanes_per_group):
    t = np.arange(seq)
    row, col = t // GRID_W, t % GRID_W
    quarter = head_dim // 4
    inv = ROPE_THETA ** (-np.arange(0, head_dim // 2, 2, dtype=np.float64) / (head_dim // 2))
    lane = np.arange(LANES) % lanes_per_group
    is_col = (lane // (head_dim // 2)) % 2 == 1
    w = lane % (head_dim // 2)
    freq = inv[w % quarter]
    pos = np.where(is_col[None, :], col[:, None], row[:, None]).astype(np.float64)
    ang = pos * freq[None, :]
    sign = np.where(w < quarter, -1.0, 1.0)[None, :]
    cos = np.concatenate([np.ones((ctx_len, LANES)), np.cos(ang)], axis=0)
    sin = np.concatenate([np.zeros((ctx_len, LANES)), np.sin(ang) * sign], axis=0)
    return jnp.asarray(cos, F32), jnp.asarray(sin, F32)


def _deinterleave_matrix():
    p = np.zeros((2 * LANES, 2 * LANES), np.float32)
    i = np.arange(LANES)
    p[2 * i, i] = 1.0
    p[2 * i + 1, LANES + i] = 1.0
    return jnp.asarray(p, BF16)


def kernel(x, c, ctx, c_ctx, w_mod, b_mod, g_norm1, g_norm2, w_in, g_qnorm, g_knorm, lambda_q1, lambda_k1,
           lambda_q2, lambda_k2, g_subln, w_pool, pool_scale, w_out, w_router, b_router, w_expert_in,
           b_expert_in, w_expert_out, b_expert_out, g_final):
    b, seq, d = x.shape
    ctx_len = ctx.shape[1]
    depth = w_mod.shape[0]
    assert ctx_len == TM and seq % TM == 0

    xa = jnp.concatenate([ctx, x], axis=1)
    cc = jnp.concatenate([c, c_ctx[None, :], jnp.zeros((8 - b - 1, d), F32)], axis=0)
    mods = _modulation(cc, w_mod, b_mod)
    rope_a = _rope_tables(ctx_len, seq, HEAD_DIM, HEAD_DIM)
    rope_b = _rope_tables(ctx_len, seq, B_QK_DIM, B_QK_DIM)
    e, _, f2 = w_expert_in.shape[1:]
    w1p_all = _deinterleave_weights(w_expert_in.reshape(depth * e, d, f2), _deinterleave_matrix())
    b1p_all = jnp.concatenate([b_expert_in[..., 0::2], b_expert_in[..., 1::2]], axis=-1).reshape(depth * e, 1, f2)
    w2_all = w_expert_out.reshape(depth * e, f2 // 2, d)
    b2_all = b_expert_out.reshape(depth * e, 1, d)

    for l in range(depth):
        last = l == depth - 1
        q0 = 1 if last else 0
        lam_init = 0.8 - 0.6 * math.exp(-0.3 * l)
        mod3 = mods[l].reshape(8, 1, 6 * d)
        qa, ka, vat, qb, kb, vbt, uc = _in_projection(
            xa, mod3, g_norm1[l][None], w_in[l].astype(BF16), g_qnorm[l][None], g_knorm[l][None],
            rope_a + rope_b)
        oa = _attention_a(qa, ka, vat, q0)
        lam_vecs = jnp.stack([lambda_q1[l], lambda_k1[l], lambda_q2[l], lambda_k2[l]])
        ob = _attention_b(qb, kb, vbt, lam_vecs, g_subln[l][:, None], q0, lam_init)
        oc = _pooling(uc, w_pool[l].astype(BF16), pool_scale[l][None], q0, ctx_len)

        wr = jnp.pad(w_router[l], ((0, 0), (0, LANES - e)))
        wr_hi = wr.astype(BF16)
        wr2 = jnp.concatenate([wr_hi, (wr - wr_hi.astype(F32)).astype(BF16)], axis=1)
        br = jnp.pad(b_router[l], (0, LANES - e))[None]
        xmid, f_moe, top_e, rank, gate_t, counts = _post_attention(
            oa, ob, oc, xa, mod3, g_norm2[l][None], w_out[l].astype(BF16), wr2, br, q0)

        ntok = top_e.shape[1]
        n_tiles = (TOP_K * ntok) // E_TM + e
        pos_flat, tile_expert, tile_block, tile_rows, tile_end = _routing_tables(top_e, rank, counts, n_tiles)
        xs = _dispatch(pos_flat, tile_end, f_moe, n_tiles * E_TM)
        ys = _experts(tile_expert + l * e, tile_block, tile_rows, xs, w1p_all, b1p_all, w2_all, b2_all)
        xa = _combine(pos_flat, ys, gate_t, xmid, mod3, g_final[None], q0, last)
    return xa
```

```python
import functools
import math

import numpy as np
import jax
import jax.numpy as jnp
from jax import lax
from jax.experimental import pallas as pl
from jax.experimental.pallas import tpu as pltpu

F32 = jnp.float32
BF16 = jnp.bfloat16
U32 = jnp.uint32
I32 = jnp.int32

HEAD_DIM = 128
A_HEADS = 8
A_KV_HEADS = 2
GQA_GROUP = A_HEADS // A_KV_HEADS
B_HEADS = 4
B_QK_DIM = 64
POOL_WINDOWS = (2, 4, 8, 16)
POOL_GROUP = 128
C_WIDTH = len(POOL_WINDOWS) * POOL_GROUP
A_Q = A_HEADS * HEAD_DIM
A_KV = A_KV_HEADS * HEAD_DIM
B_QK = B_HEADS * 2 * B_QK_DIM
B_V = B_HEADS * 2 * B_QK_DIM
N_EXPERTS = 32
TOP_K = 4
SWIGLU_LIMIT = 7.0
SWIGLU_ALPHA = 1.702
ROPE_THETA = 10000.0
GRID_W = 64
EPS = 1e-6

TM = 256
KV_TILE = 4096
BF16_ROWS = 16
VT_ROWS = 128 + BF16_ROWS
A_STREAM_HEADS = 2
E_TM = 512
POOL_HALO = 8
LANES = 128
SUBLANES = 8
VMEM_LIMIT = 56 * 1024 * 1024
LOG2E = math.log2(math.e)
NEG_BIG = -1e30


def _cparams(sem, vmem=None):
    return pltpu.CompilerParams(dimension_semantics=sem, vmem_limit_bytes=vmem)


def _sigmoid(z):
    return 1.0 / (1.0 + jnp.exp(-z))


def _mod_kernel(cc_ref, w_ref, b_ref, o_ref):
    cc = cc_ref[...]
    a = (cc * _sigmoid(cc)).astype(BF16)
    o_ref[0] = jnp.dot(a, w_ref[0].astype(BF16), preferred_element_type=F32) + b_ref[0]


def _modulation(cc, w_mod, b_mod):
    depth, d, n = w_mod.shape
    tn = 1536
    return pl.pallas_call(
        _mod_kernel,
        grid=(depth, n // tn),
        in_specs=[
            pl.BlockSpec((8, d), lambda l, j: (0, 0)),
            pl.BlockSpec((1, d, tn), lambda l, j: (l, 0, j)),
            pl.BlockSpec((1, 1, tn), lambda l, j: (l, 0, j)),
        ],
        out_specs=pl.BlockSpec((1, 8, tn), lambda l, j: (l, 0, j)),
        out_shape=jax.ShapeDtypeStruct((depth, 8, n), F32),
        compiler_params=_cparams(("parallel", "parallel"), VMEM_LIMIT),
        name="modulation",
    )(cc, w_mod, b_mod.reshape(depth, 1, n))


def _rope(y, cos, sin_signed, half):
    width = y.shape[-1]
    lane = lax.broadcasted_iota(I32, y.shape, 1)
    partner = jnp.where(lane % (2 * half) < half,
                        pltpu.roll(y, width - half, 1), pltpu.roll(y, half, 1))
    return y * cos + partner * sin_signed


def _head_rms(y, g):
    return y * lax.rsqrt(jnp.mean(y * y, axis=-1, keepdims=True) + EPS) * g


def _proj_kernel(x_ref, mod_ref, g1_ref, w_ref, gq_ref, gk_ref, ca_ref, sa_ref, cb_ref, sb_ref,
                 qa_ref, ka_ref, vat_ref, qb_ref, kb_ref, vbt_ref, uc_ref):
    d = x_ref.shape[-1]
    x = x_ref[0]
    xn = x * lax.rsqrt(jnp.mean(x * x, axis=-1, keepdims=True) + EPS) * g1_ref[...]
    sh = mod_ref[0, :, 0:d]
    sc = mod_ref[0, :, d:2 * d]
    h = (xn * (1.0 + sc) + sh).astype(BF16)
    ca, sa, cb, sb = ca_ref[...], sa_ref[...], cb_ref[...], sb_ref[...]

    qa_scale = LOG2E / math.sqrt(HEAD_DIM)
    qb_scale = LOG2E / math.sqrt(B_QK_DIM)
    ones = jnp.ones((BF16_ROWS, x.shape[0]), BF16)

    def store_vt(ref, hd, v):
        ref[0, hd * VT_ROWS:hd * VT_ROWS + HEAD_DIM, :] = v.T.astype(BF16)
        ref[0, hd * VT_ROWS + HEAD_DIM:(hd + 1) * VT_ROWS, :] = ones

    c0 = 0
    pq = jnp.dot(h, w_ref[:, c0:c0 + A_Q], preferred_element_type=F32)
    for hd in range(A_HEADS):
        y = _head_rms(pq[:, hd * HEAD_DIM:(hd + 1) * HEAD_DIM], gq_ref[...])
        qa_ref[0, :, hd * HEAD_DIM:(hd + 1) * HEAD_DIM] = (
            _rope(y, ca, sa, HEAD_DIM // 4) * qa_scale).astype(BF16)
    c0 += A_Q
    pkv = jnp.dot(h, w_ref[:, c0:c0 + 2 * A_KV], preferred_element_type=F32)
    for hd in range(A_KV_HEADS):
        y = _head_rms(pkv[:, hd * HEAD_DIM:(hd + 1) * HEAD_DIM], gk_ref[...])
        ka_ref[0, :, hd * HEAD_DIM:(hd + 1) * HEAD_DIM] = _rope(y, ca, sa, HEAD_DIM // 4).astype(BF16)
        store_vt(vat_ref, hd, pkv[:, A_KV + hd * HEAD_DIM:A_KV + (hd + 1) * HEAD_DIM])
    c0 += 2 * A_KV
    pb = jnp.dot(h, w_ref[:, c0:c0 + 2 * B_QK], preferred_element_type=F32)
    lane = lax.broadcasted_iota(I32, (x.shape[0], LANES), 1)

    def head_pair(base, hd):
        v0 = pb[:, base + (hd // 2) * LANES:base + (hd // 2 + 1) * LANES]
        v1 = pb[:, base + B_QK // 2 + (hd // 2) * LANES:base + B_QK // 2 + (hd // 2 + 1) * LANES]
        if hd % 2 == 0:
            return jnp.where(lane < B_QK_DIM, v0, pltpu.roll(v1, B_QK_DIM, 1))
        return jnp.where(lane < B_QK_DIM, pltpu.roll(v0, B_QK_DIM, 1), v1)

    for hd in range(B_HEADS):
        sl = slice(hd * LANES, (hd + 1) * LANES)
        qb_ref[0, :, sl] = (_rope(head_pair(0, hd), cb, sb, B_QK_DIM // 4) * qb_scale).astype(BF16)
        kb_ref[0, :, sl] = _rope(head_pair(B_QK, hd), cb, sb, B_QK_DIM // 4).astype(BF16)
    c0 += 2 * B_QK
    pvu = jnp.dot(h, w_ref[:, c0:c0 + B_V + C_WIDTH], preferred_element_type=F32)
    for hd in range(B_HEADS):
        store_vt(vbt_ref, hd, pvu[:, hd * HEAD_DIM:(hd + 1) * HEAD_DIM])
    uc_ref[0] = pvu[:, B_V:B_V + C_WIDTH]


def _in_projection(xa, mod3, g1, w_in_p, gq, gk, rope):
    b, t, d = xa.shape
    nt = t // TM
    n_in = w_in_p.shape[1]
    tok = lambda w: pl.BlockSpec((1, TM, w), lambda bb, i: (bb, i, 0))
    tok_t = lambda w: pl.BlockSpec((1, w, TM), lambda bb, i: (bb, 0, i))
    const2 = lambda r, c: pl.BlockSpec((r, c), lambda bb, i: (0, 0))
    tab = pl.BlockSpec((TM, LANES), lambda bb, i: (i, 0))
    return pl.pallas_call(
        _proj_kernel,
        grid=(b, nt),
        in_specs=[
            tok(d),
            pl.BlockSpec((1, 1, mod3.shape[-1]), lambda bb, i: (jnp.where(i == 0, b, bb), 0, 0)),
            const2(1, d), const2(d, n_in), const2(1, HEAD_DIM), const2(1, HEAD_DIM),
            tab, tab, tab, tab,
        ],
        out_specs=[tok(A_Q), tok(A_KV), tok_t(A_KV_HEADS * VT_ROWS), tok(B_QK), tok(B_QK),
                   tok_t(B_HEADS * VT_ROWS), tok(C_WIDTH)],
        out_shape=[
            jax.ShapeDtypeStruct((b, t, A_Q), BF16),
            jax.ShapeDtypeStruct((b, t, A_KV), BF16),
            jax.ShapeDtypeStruct((b, A_KV_HEADS * VT_ROWS, t), BF16),
            jax.ShapeDtypeStruct((b, t, B_QK), BF16),
            jax.ShapeDtypeStruct((b, t, B_QK), BF16),
            jax.ShapeDtypeStruct((b, B_HEADS * VT_ROWS, t), BF16),
            jax.ShapeDtypeStruct((b, t, C_WIDTH), F32),
        ],
        compiler_params=_cparams(("parallel", "parallel"), VMEM_LIMIT),
        name="in_projection",
    )(xa, mod3, g1, w_in_p, gq, gk, *rope)


def _flash_t(q_stacks, kv_of, k_ref, vt_ref, m_ref, acc_ref, n_kv):
    def step(off, size):
        ss = [lax.dot_general(k_ref[0, pl.ds(off, size), kv * LANES:(kv + 1) * LANES], q,
                              (((1,), (1,)), ((), ())), preferred_element_type=F32)
              for kv, q in zip(kv_of, q_stacks)]
        for h, (kv, s) in enumerate(zip(kv_of, ss)):
            m = m_ref[h]
            m_new = jnp.maximum(m, jnp.max(s, axis=0, keepdims=True))
            p = jnp.exp2(s - m_new)
            vt = vt_ref[0, kv * VT_ROWS:(kv + 1) * VT_ROWS, pl.ds(off, size)]
            acc_ref[h] = (jnp.exp2(m - m_new) * acc_ref[h]
                          + jnp.dot(vt, p.astype(BF16), preferred_element_type=F32))
            m_ref[h] = m_new

    m_ref[...] = jnp.full(m_ref.shape, NEG_BIG, F32)
    acc_ref[...] = jnp.zeros(acc_ref.shape, F32)
    step(0, TM)

    def body(j, carry):
        step(pl.multiple_of(TM + j * KV_TILE, TM), KV_TILE)
        return carry

    lax.fori_loop(0, n_kv, body, 0)
    return [acc_ref[h, 0:HEAD_DIM] / acc_ref[h, HEAD_DIM:HEAD_DIM + 1] for h in range(len(q_stacks))]


def _attn_a_kernel(q_ref, k_ref, vt_ref, o_ref, m_ref, acc_ref, *, q0, n_kv_full):
    qi = pl.program_id(1) + q0
    n_kv = jnp.where(qi == 0, 0, n_kv_full)
    q = q_ref[0]
    n_streams = A_HEADS // A_STREAM_HEADS
    q_stacks = [jnp.concatenate([q[:, (s * A_STREAM_HEADS + g) * HEAD_DIM:(s * A_STREAM_HEADS + g + 1) * HEAD_DIM]
                                 for g in range(A_STREAM_HEADS)], axis=0) for s in range(n_streams)]
    kv_of = [s * A_STREAM_HEADS // GQA_GROUP for s in range(n_streams)]
    for s, o in enumerate(_flash_t(q_stacks, kv_of, k_ref, vt_ref, m_ref, acc_ref, n_kv)):
        for g in range(A_STREAM_HEADS):
            c0 = (s * A_STREAM_HEADS + g) * HEAD_DIM
            o_ref[0, :, c0:c0 + HEAD_DIM] = o[:, g * TM:(g + 1) * TM].T.astype(BF16)


def _attn_b_kernel(q_ref, k_ref, vt_ref, lam_ref, gs_ref, o_ref, m_ref, acc_ref, *, q0, n_kv_full, lam_init):
    qi = pl.program_id(1) + q0
    n_kv = jnp.where(qi == 0, 0, n_kv_full)
    lane = lax.broadcasted_iota(I32, (TM, LANES), 1)
    zero = jnp.zeros((TM, LANES), BF16)
    q_stacks = []
    for h in range(B_HEADS):
        q = q_ref[0, :, h * LANES:(h + 1) * LANES]
        q_stacks.append(jnp.concatenate(
            [jnp.where(lane < B_QK_DIM, q, zero), jnp.where(lane >= B_QK_DIM, q, zero)], axis=0))
    res = _flash_t(q_stacks, list(range(B_HEADS)), k_ref, vt_ref, m_ref, acc_ref, n_kv)
    lv = lam_ref[...]
    lam = (jnp.exp(jnp.sum(lv[0:1] * lv[1:2], axis=-1, keepdims=True))
           - jnp.exp(jnp.sum(lv[2:3] * lv[3:4], axis=-1, keepdims=True)) + lam_init)
    for h, o in enumerate(res):
        od = o[:, 0:TM] - lam * o[:, TM:2 * TM]
        r = lax.rsqrt(jnp.mean(od * od, axis=0, keepdims=True) + EPS)
        y = od * r * gs_ref[...] * (1.0 - lam_init)
        o_ref[0, :, h * LANES:(h + 1) * LANES] = y.T.astype(BF16)


def _attention_a(qa, ka, vat, q0):
    b, t, _ = qa.shape
    nt = t // TM
    return pl.pallas_call(
        functools.partial(_attn_a_kernel, q0=q0, n_kv_full=(t - TM) // KV_TILE),
        grid=(b, nt - q0),
        in_specs=[
            pl.BlockSpec((1, TM, A_Q), lambda bb, i: (bb, i + q0, 0)),
            pl.BlockSpec((1, t, A_KV), lambda bb, i: (bb, 0, 0)),
            pl.BlockSpec((1, A_KV_HEADS * VT_ROWS, t), lambda bb, i: (bb, 0, 0)),
        ],
        out_specs=pl.BlockSpec((1, TM, A_Q), lambda bb, i: (bb, i, 0)),
        out_shape=jax.ShapeDtypeStruct((b, t - q0 * TM, A_Q), BF16),
        scratch_shapes=[pltpu.VMEM((A_HEADS // A_STREAM_HEADS, 1, A_STREAM_HEADS * TM), F32),
                        pltpu.VMEM((A_HEADS // A_STREAM_HEADS, VT_ROWS, A_STREAM_HEADS * TM), F32)],
        compiler_params=_cparams(("parallel", "parallel"), VMEM_LIMIT),
        name="attention_gqa",
    )(qa, ka, vat)


def _attention_b(qb, kb, vbt, lam_vecs, g_sub_col, q0, lam_init):
    b, t, _ = qb.shape
    nt = t // TM
    return pl.pallas_call(
        functools.partial(_attn_b_kernel, q0=q0, n_kv_full=(t - TM) // KV_TILE, lam_init=lam_init),
        grid=(b, nt - q0),
        in_specs=[
            pl.BlockSpec((1, TM, B_QK), lambda bb, i: (bb, i + q0, 0)),
            pl.BlockSpec((1, t, B_QK), lambda bb, i: (bb, 0, 0)),
            pl.BlockSpec((1, B_HEADS * VT_ROWS, t), lambda bb, i: (bb, 0, 0)),
            pl.BlockSpec((4, B_QK_DIM), lambda bb, i: (0, 0)),
            pl.BlockSpec((LANES, 1), lambda bb, i: (0, 0)),
        ],
        out_specs=pl.BlockSpec((1, TM, B_V), lambda bb, i: (bb, i, 0)),
        out_shape=jax.ShapeDtypeStruct((b, t - q0 * TM, B_V), BF16),
        scratch_shapes=[pltpu.VMEM((B_HEADS, 1, 2 * TM), F32), pltpu.VMEM((B_HEADS, VT_ROWS, 2 * TM), F32)],
        compiler_params=_cparams(("parallel", "parallel"), VMEM_LIMIT),
        name="attention_diff",
    )(qb, kb, vbt, lam_vecs, g_sub_col)


def _pool_kernel(prev_ref, cur_ref, next_ref, wp_ref, ps_ref, o_ref, *, q0, ctx_len, total_len):
    i = pl.program_id(1) + q0
    ext = jnp.concatenate([prev_ref[0], cur_ref[0], next_ref[0]], axis=0)
    rows = ext.shape[0]
    seg_lo = jnp.where(i == 0, 0, ctx_len)
    seg_hi = jnp.where(i == 0, ctx_len, total_len)
    grow = i * TM - POOL_HALO + lax.broadcasted_iota(I32, (rows, 1), 0)
    ext = jnp.where((grow >= seg_lo) & (grow < seg_hi), ext, 0.0)
    tpos = grow[POOL_HALO:POOL_HALO + TM] - seg_lo
    seg_len = seg_hi - seg_lo

    def back(a, d):
        return pltpu.roll(a, d, 0)

    def fwd(a, d):
        return pltpu.roll(a, rows - d, 0)

    for gi, w in enumerate(POOL_WINDOWS):
        u = ext[:, gi * POOL_GROUP:(gi + 1) * POOL_GROUP]
        acc = u + back(u, 1)
        span = 2
        while span < w:
            acc = acc + back(acc, span)
            span *= 2
        if w > 2:
            acc = fwd(acc, w // 2 - 1)
        cnt = (jnp.minimum(tpos + w // 2, seg_len) - jnp.maximum(tpos - w // 2, 0)).astype(F32)
        sl = slice(POOL_HALO, POOL_HALO + TM)
        p = (acc[sl] / cnt - u[sl]).astype(BF16)
        y = jnp.dot(p, wp_ref[gi], preferred_element_type=F32)
        o_ref[0, :, gi * POOL_GROUP:(gi + 1) * POOL_GROUP] = (
            y * ps_ref[:, gi * POOL_GROUP:(gi + 1) * POOL_GROUP]).astype(BF16)


def _pooling(uc, w_pool_b, pool_scale, q0, ctx_len):
    b, t, cw = uc.shape
    nt = t // TM
    per = TM // POOL_HALO
    last = t // POOL_HALO - 1
    return pl.pallas_call(
        functools.partial(_pool_kernel, q0=q0, ctx_len=ctx_len, total_len=t),
        grid=(b, nt - q0),
        in_specs=[
            pl.BlockSpec((1, POOL_HALO, cw), lambda bb, i: (bb, jnp.maximum((i + q0) * per - 1, 0), 0)),
            pl.BlockSpec((1, TM, cw), lambda bb, i: (bb, i + q0, 0)),
            pl.BlockSpec((1, POOL_HALO, cw), lambda bb, i: (bb, jnp.minimum((i + q0 + 1) * per, last), 0)),
            pl.BlockSpec(w_pool_b.shape, lambda bb, i: (0, 0, 0)),
            pl.BlockSpec((1, cw), lambda bb, i: (0, 0)),
        ],
        out_specs=pl.BlockSpec((1, TM, cw), lambda bb, i: (bb, i, 0)),
        out_shape=jax.ShapeDtypeStruct((b, t - q0 * TM, cw), BF16),
        compiler_params=_cparams(("parallel", "parallel")),
        name="pooling",
    )(uc, uc, uc, w_pool_b, pool_scale)


def _post_kernel(oa_ref, ob_ref, oc_ref, x_ref, mod_ref, g2_ref, wo_ref, wr_ref, br_ref,
                 xmid_ref, f_ref, te_ref, rk_ref, gt_ref, cnt_ref, carry_ref):
    d = x_ref.shape[-1]
    first = (pl.program_id(0) == 0) & (pl.program_id(1) == 0)

    @pl.when(first)
    def _():
        carry_ref[...] = jnp.zeros_like(carry_ref)

    m = (jnp.dot(oa_ref[0], wo_ref[0:A_Q], preferred_element_type=F32)
         + jnp.dot(ob_ref[0], wo_ref[A_Q:A_Q + B_V], preferred_element_type=F32)
         + jnp.dot(oc_ref[0], wo_ref[A_Q + B_V:A_Q + B_V + C_WIDTH], preferred_element_type=F32))
    x = x_ref[0] + mod_ref[0, :, 2 * d:3 * d] * m
    xmid_ref[0] = x
    xn = x * lax.rsqrt(jnp.mean(x * x, axis=-1, keepdims=True) + EPS) * g2_ref[...]
    f = xn * (1.0 + mod_ref[0, :, 4 * d:5 * d]) + mod_ref[0, :, 3 * d:4 * d]
    f_ref[0] = f

    f_hi = f.astype(BF16)
    f_lo = (f - f_hi.astype(F32)).astype(BF16)
    lg2 = jnp.dot(f_hi, wr_ref[...], preferred_element_type=F32)
    lg1 = jnp.dot(f_lo, wr_ref[:, 0:LANES], preferred_element_type=F32)
    logits = lg2[:, 0:LANES] + lg2[:, LANES:2 * LANES] + lg1 + br_ref[...]
    work = logits.T[0:N_EXPERTS]

    e_id = lax.broadcasted_iota(I32, work.shape, 0).astype(F32)
    vals, idxs, hots = [], [], []
    for _ in range(TOP_K):
        mx = jnp.max(work, axis=0, keepdims=True)
        idx = jnp.min(jnp.where(work == mx, e_id, float(N_EXPERTS)), axis=0, keepdims=True)
        hot = e_id == idx
        vals.append(mx)
        idxs.append(idx)
        hots.append(hot)
        work = jnp.where(hot, -jnp.inf, work)
    ex = [jnp.exp(v - vals[0]) for v in vals]
    den = ex[0] + ex[1] + ex[2] + ex[3]
    gates = [e / den for e in ex]

    msel = jnp.zeros(work.shape, F32)
    for hot in hots:
        msel = msel + jnp.where(hot, 1.0, 0.0)
    r_i = lax.broadcasted_iota(I32, (TM, TM), 0)
    c_i = lax.broadcasted_iota(I32, (TM, TM), 1)
    tri = jnp.where(r_i < c_i, 1.0, 0.0).astype(BF16)
    carry = carry_ref[:, 0:1]
    rank_full = jnp.dot(msel.astype(BF16), tri, preferred_element_type=F32) + carry
    ranks = [jnp.sum(jnp.where(hot, rank_full, 0.0), axis=0, keepdims=True) for hot in hots]
    new_carry = carry + jnp.sum(msel, axis=1, keepdims=True)
    carry_ref[...] = jnp.broadcast_to(new_carry, carry_ref.shape)
    cnt_ref[...] = jnp.broadcast_to(new_carry, cnt_ref.shape)

    row8 = lax.broadcasted_iota(I32, (8, TM), 0)

    def rows8(vs):
        out = jnp.zeros((8, TM), F32)
        for k, v in enumerate(vs):
            out = jnp.where(row8 == k, v, out)
        return out

    te_ref[...] = rows8(idxs).astype(I32)
    rk_ref[...] = rows8(ranks).astype(I32)
    g128 = jnp.concatenate([rows8(gates), jnp.zeros((LANES - 8, TM), F32)], axis=0)
    gt_ref[...] = g128.T


def _post_attention(oa, ob, oc, xa, mod3, g2, w_out_b, wr2, br, q0):
    b, t, d = xa.shape
    nt = t // TM - q0
    ntok = b * nt * TM
    tok = lambda w: pl.BlockSpec((1, TM, w), lambda bb, i: (bb, i, 0))
    const2 = lambda r, c: pl.BlockSpec((r, c), lambda bb, i: (0, 0))
    flat = lambda r: pl.BlockSpec((r, TM), lambda bb, i: (0, bb * nt + i))
    return pl.pallas_call(
        _post_kernel,
        grid=(b, nt),
        in_specs=[
            tok(A_Q), tok(B_V), tok(C_WIDTH),
            pl.BlockSpec((1, TM, d), lambda bb, i: (bb, i + q0, 0)),
            pl.BlockSpec((1, 1, mod3.shape[-1]), lambda bb, i: (jnp.where(i + q0 == 0, b, bb), 0, 0)),
            const2(1, d), const2(d, d), const2(d, 2 * LANES), const2(1, LANES),
        ],
        out_specs=[
            tok(d), tok(d), flat(8), flat(8),
            pl.BlockSpec((TM, LANES), lambda bb, i: (bb * nt + i, 0)),
            const2(N_EXPERTS, LANES),
        ],
        out_shape=[
            jax.ShapeDtypeStruct((b, nt * TM, d), F32),
            jax.ShapeDtypeStruct((b, nt * TM, d), F32),
            jax.ShapeDtypeStruct((8, ntok), I32),
            jax.ShapeDtypeStruct((8, ntok), I32),
            jax.ShapeDtypeStruct((ntok, LANES), F32),
            jax.ShapeDtypeStruct((N_EXPERTS, LANES), F32),
        ],
        scratch_shapes=[pltpu.VMEM((N_EXPERTS, LANES), F32)],
        compiler_params=_cparams(("arbitrary", "arbitrary"), VMEM_LIMIT),
        name="out_projection_router",
    )(oa, ob, oc, xa, mod3, g2, w_out_b, wr2, br)


def _row_copies(pos_ref, src_row, dst_row, sem):
    def issue(i, carry):
        base = pl.multiple_of(i * SUBLANES, SUBLANES)
        for j in range(SUBLANES):
            for k in range(TOP_K):
                p = pos_ref[k * TM + base + j]
                pltpu.make_async_copy(src_row(k, base + j, p), dst_row(k, base + j, p), sem).start(priority=k % 2)
        return carry

    lax.fori_loop(0, TM // SUBLANES, issue, 0)


def _dispatch_kernel(pos_ref, tend_ref, f_ref, xs_ref, zbuf, sem, zsem):
    first = (pl.program_id(0) == 0) & (pl.program_id(1) == 0)

    @pl.when(first)
    def _():
        zbuf[...] = jnp.zeros_like(zbuf)

        def last_tile_copy(e):
            return pltpu.make_async_copy(zbuf, xs_ref.at[pl.ds((tend_ref[e] - 1) * E_TM, E_TM)], zsem)

        def has_tiles(e):
            return tend_ref[e] > (tend_ref[e - 1] if e else 0)

        def tail_copy(i):
            return pltpu.make_async_copy(zbuf, xs_ref.at[pl.ds(i * E_TM, E_TM)], zsem)

        n_used, n_tiles = tend_ref[N_EXPERTS - 1], xs_ref.shape[0] // E_TM
        for e in range(N_EXPERTS):
            pl.when(has_tiles(e))(lambda e=e: last_tile_copy(e).start())
        lax.fori_loop(n_used, n_tiles, lambda i, c: (tail_copy(i).start(), c)[1], 0)
        for e in range(N_EXPERTS):
            pl.when(has_tiles(e))(lambda e=e: last_tile_copy(e).wait())
        lax.fori_loop(n_used, n_tiles, lambda i, c: (tail_copy(i).wait(), c)[1], 0)

    _row_copies(pos_ref,
                lambda k, t, p: f_ref.at[0, pl.ds(t, 1)],
                lambda k, t, p: xs_ref.at[pl.ds(p, 1)], sem)
    for k in range(TOP_K):
        pltpu.make_async_copy(f_ref.at[0], xs_ref.at[pl.ds(0, TM)], sem).wait()


def _dispatch(pos_flat, tile_end, f, n_rows):
    b, t, hw = f.shape
    nt = t // TM
    return pl.pallas_call(
        _dispatch_kernel,
        grid=(b, nt),
        in_specs=[
            pl.BlockSpec((TOP_K * TM,), lambda bb, i: (bb * nt + i,), memory_space=pltpu.SMEM),
            pl.BlockSpec(memory_space=pltpu.SMEM),
            pl.BlockSpec((1, TM, hw), lambda bb, i: (bb, i, 0)),
        ],
        out_specs=pl.BlockSpec(memory_space=pl.ANY),
        out_shape=jax.ShapeDtypeStruct((n_rows, hw), F32),
        scratch_shapes=[pltpu.VMEM((E_TM, hw), F32), pltpu.SemaphoreType.DMA(()), pltpu.SemaphoreType.DMA(())],
        compiler_params=_cparams(("arbitrary", "arbitrary")),
        name="moe_dispatch",
    )(pos_flat, tile_end, f)


def _deinterleave_kernel(w_ref, p_ref, o_ref):
    fdim = w_ref.shape[2] // 2
    for j in range(fdim // LANES):
        chunk = w_ref[0, :, 2 * LANES * j:2 * LANES * (j + 1)].astype(BF16)
        r = jnp.dot(chunk, p_ref[...], preferred_element_type=F32)
        o_ref[0, :, LANES * j:LANES * (j + 1)] = r[:, 0:LANES].astype(BF16)
        o_ref[0, :, fdim + LANES * j:fdim + LANES * (j + 1)] = r[:, LANES:2 * LANES].astype(BF16)


def _deinterleave_weights(w, perm):
    n, d, f2 = w.shape
    tk = 512
    return pl.pallas_call(
        _deinterleave_kernel,
        grid=(n, d // tk),
        in_specs=[
            pl.BlockSpec((1, tk, f2), lambda e, j: (e, j, 0)),
            pl.BlockSpec(perm.shape, lambda e, j: (0, 0)),
        ],
        out_specs=pl.BlockSpec((1, tk, f2), lambda e, j: (e, j, 0)),
        out_shape=jax.ShapeDtypeStruct((n, d, f2), BF16),
        compiler_params=_cparams(("parallel", "parallel")),
        name="expert_weight_layout",
    )(w, perm)


def _experts_kernel(te_ref, tb_ref, rows_ref, x_ref, w1_ref, b1_ref, w2_ref, b2_ref, y_ref, w2b_ref):
    del tb_ref
    fdim = w2_ref.shape[1]
    i = pl.program_id(0)
    used = rows_ref[i] > 0

    @pl.when(jnp.logical_not(used))
    def _():
        y_ref[...] = jnp.zeros_like(y_ref)

    @pl.when(used & ((i == 0) | (te_ref[i] != te_ref[jnp.maximum(i - 1, 0)])))
    def _():
        w2b_ref[...] = w2_ref[0].astype(BF16)

    @pl.when(used)
    def _():
        u = jnp.dot(x_ref[...].astype(BF16), w1_ref[0], preferred_element_type=F32) + b1_ref[0]
        glu = jnp.minimum(u[:, 0:fdim], SWIGLU_LIMIT)
        lin = jnp.clip(u[:, fdim:2 * fdim], -SWIGLU_LIMIT, SWIGLU_LIMIT)
        a = glu * _sigmoid(SWIGLU_ALPHA * glu) * (lin + 1.0)
        y_ref[...] = jnp.dot(a.astype(BF16), w2b_ref[...], preferred_element_type=F32) + b2_ref[0]


def _experts(tile_expert, tile_block, tile_rows, xs, w1p, b1p, w2, b2):
    n_rows, hw = xs.shape
    n_tiles = n_rows // E_TM
    _, d, f2 = w1p.shape
    fdim = f2 // 2
    grid_spec = pltpu.PrefetchScalarGridSpec(
        num_scalar_prefetch=3,
        grid=(n_tiles,),
        in_specs=[
            pl.BlockSpec((E_TM, hw), lambda i, te, tb, nu: (tb[i], 0)),
            pl.BlockSpec((1, d, f2), lambda i, te, tb, nu: (te[i], 0, 0)),
            pl.BlockSpec((1, 1, f2), lambda i, te, tb, nu: (te[i], 0, 0)),
            pl.BlockSpec((1, fdim, d), lambda i, te, tb, nu: (te[i], 0, 0)),
            pl.BlockSpec((1, 1, d), lambda i, te, tb, nu: (te[i], 0, 0)),
        ],
        out_specs=pl.BlockSpec((E_TM, hw), lambda i, te, tb, nu: (i, 0)),
        scratch_shapes=[pltpu.VMEM((fdim, d), BF16)],
    )
    return pl.pallas_call(
        _experts_kernel,
        grid_spec=grid_spec,
        out_shape=jax.ShapeDtypeStruct((n_rows, hw), F32),
        compiler_params=_cparams(("arbitrary",), 60 * 1024 * 1024),
        name="moe_experts",
    )(tile_expert, tile_block, tile_rows, xs, w1p, b1p, w2, b2)


def _combine_kernel(pos_ref, ys_ref, gt_ref, x_ref, mod_ref, gf_ref, o_ref, ybuf, sem, *, final):
    d = x_ref.shape[-1]
    _row_copies(pos_ref,
                lambda k, t, p: ys_ref.at[pl.ds(p, 1)],
                lambda k, t, p: ybuf.at[k, pl.ds(t, 1)], sem)
    for k in range(TOP_K):
        pltpu.make_async_copy(ys_ref.at[pl.ds(0, TM)], ybuf.at[k], sem).wait()
    acc = gt_ref[:, 0:1] * ybuf[0]
    for k in range(1, TOP_K):
        acc = acc + gt_ref[:, k:k + 1] * ybuf[k]
    xo = x_ref[0] + mod_ref[0, :, 5 * d:6 * d] * acc
    if final:
        xo = xo * lax.rsqrt(jnp.mean(xo * xo, axis=-1, keepdims=True) + EPS) * gf_ref[...]
    o_ref[0] = xo


def _combine(pos_flat, ys, gate_t, xmid, mod3, g_final, q0, final):
    b, t, d = xmid.shape
    nt = t // TM
    return pl.pallas_call(
        functools.partial(_combine_kernel, final=final),
        grid=(b, nt),
        in_specs=[
            pl.BlockSpec((TOP_K * TM,), lambda bb, i: (bb * nt + i,), memory_space=pltpu.SMEM),
            pl.BlockSpec(memory_space=pl.ANY),
            pl.BlockSpec((TM, LANES), lambda bb, i: (bb * nt + i, 0)),
            pl.BlockSpec((1, TM, d), lambda bb, i: (bb, i, 0)),
            pl.BlockSpec((1, 1, mod3.shape[-1]), lambda bb, i: (jnp.where(i + q0 == 0, b, bb), 0, 0)),
            pl.BlockSpec((1, d), lambda bb, i: (0, 0)),
        ],
        out_specs=pl.BlockSpec((1, TM, d), lambda bb, i: (bb, i, 0)),
        out_shape=jax.ShapeDtypeStruct((b, t, d), F32),
        scratch_shapes=[pltpu.VMEM((TOP_K, TM, d), F32), pltpu.SemaphoreType.DMA(())],
        compiler_params=_cparams(("arbitrary", "arbitrary"), VMEM_LIMIT),
        name="moe_combine",
    )(pos_flat, ys, gate_t, xmid, mod3, g_final)


def _routing_tables(top_e, rank, counts, n_tiles):
    cnt = counts[:, 0].astype(I32)
    tiles_e = (cnt + E_TM - 1) // E_TM
    tile_end = jnp.cumsum(tiles_e)
    row_start = (tile_end - tiles_e) * E_TM
    hit = top_e[0:TOP_K, :, None] == jnp.arange(N_EXPERTS, dtype=I32)
    pos = jnp.sum(jnp.where(hit, row_start, 0), axis=-1) + rank[0:TOP_K]
    ntok = pos.shape[1]
    pos_flat = pos.reshape(TOP_K, ntok // TM, TM).transpose(1, 0, 2).reshape(-1)
    n_used = tile_end[-1]
    tile_id = jnp.arange(n_tiles, dtype=I32)
    ti = jnp.minimum(tile_id, n_used - 1)
    tile_expert = jnp.minimum(jnp.sum((tile_end[None, :] <= ti[:, None]).astype(I32), axis=1), N_EXPERTS - 1)
    mine = tile_expert[:, None] == jnp.arange(N_EXPERTS, dtype=I32)
    rows_left = jnp.sum(jnp.where(mine, cnt + row_start, 0), axis=1) - ti * E_TM
    tile_rows = jnp.where(tile_id < n_used, jnp.clip(rows_left, 0, E_TM), 0).astype(I32)
    return pos_flat, tile_expert, ti, tile_rows, tile_end.astype(I32)


def _rope_tables(ctx_len, seq, head_dim, lanes_per_group):
    t = np.arange(seq)
    row, col = t // GRID_W, t % GRID_W
    quarter = head_dim // 4
    inv = ROPE_THETA ** (-np.arange(0, head_dim // 2, 2, dtype=np.float64) / (head_dim // 2))
    lane = np.arange(LANES) % lanes_per_group
    is_col = (lane // (head_dim // 2)) % 2 == 1
    w = lane % (head_dim // 2)
    freq = inv[w % quarter]
    pos = np.where(is_col[None, :], col[:, None], row[:, None]).astype(np.float64)
    ang = pos * freq[None, :]
    sign = np.where(w < quarter, -1.0, 1.0)[None, :]
    cos = np.concatenate([np.ones((ctx_len, LANES)), np.cos(ang)], axis=0)
    sin = np.concatenate([np.zeros((ctx_len, LANES)), np.sin(ang) * sign], axis=0)
    return jnp.asarray(cos, F32), jnp.asarray(sin, F32)


def _deinterleave_matrix():
    p = np.zeros((2 * LANES, 2 * LANES), np.float32)
    i = np.arange(LANES)
    p[2 * i, i] = 1.0
    p[2 * i + 1, LANES + i] = 1.0
    return jnp.asarray(p, BF16)


def kernel(x, c, ctx, c_ctx, w_mod, b_mod, g_norm1, g_norm2, w_in, g_qnorm, g_knorm, lambda_q1, lambda_k1,
           lambda_q2, lambda_k2, g_subln, w_pool, pool_scale, w_out, w_router, b_router, w_expert_in,
           b_expert_in, w_expert_out, b_expert_out, g_final):
    b, seq, d = x.shape
    ctx_len = ctx.shape[1]
    depth = w_mod.shape[0]
    assert ctx_len == TM and seq % TM == 0

    xa = jnp.concatenate([ctx, x], axis=1)
    cc = jnp.concatenate([c, c_ctx[None, :], jnp.zeros((8 - b - 1, d), F32)], axis=0)
    mods = _modulation(cc, w_mod, b_mod)
    rope_a = _rope_tables(ctx_len, seq, HEAD_DIM, HEAD_DIM)
    rope_b = _rope_tables(ctx_len, seq, B_QK_DIM, B_QK_DIM)
    e, _, f2 = w_expert_in.shape[1:]
    w1p_all = _deinterleave_weights(w_expert_in.reshape(depth * e, d, f2), _deinterleave_matrix())
    b1p_all = jnp.concatenate([b_expert_in[..., 0::2], b_expert_in[..., 1::2]], axis=-1).reshape(depth * e, 1, f2)
    w2_all = w_expert_out.reshape(depth * e, f2 // 2, d)
    b2_all = b_expert_out.reshape(depth * e, 1, d)

    for l in range(depth):
        last = l == depth - 1
        q0 = 1 if last else 0
        lam_init = 0.8 - 0.6 * math.exp(-0.3 * l)
        mod3 = mods[l].reshape(8, 1, 6 * d)
        qa, ka, vat, qb, kb, vbt, uc = _in_projection(
            xa, mod3, g_norm1[l][None], w_in[l].astype(BF16), g_qnorm[l][None], g_knorm[l][None],
            rope_a + rope_b)
        oa = _attention_a(qa, ka, vat, q0)
        lam_vecs = jnp.stack([lambda_q1[l], lambda_k1[l], lambda_q2[l], lambda_k2[l]])
        ob = _attention_b(qb, kb, vbt, lam_vecs, g_subln[l][:, None], q0, lam_init)
        oc = _pooling(uc, w_pool[l].astype(BF16), pool_scale[l][None], q0, ctx_len)

        wr = jnp.pad(w_router[l], ((0, 0), (0, LANES - e)))
        wr_hi = wr.astype(BF16)
        wr2 = jnp.concatenate([wr_hi, (wr - wr_hi.astype(F32)).astype(BF16)], axis=1)
        br = jnp.pad(b_router[l], (0, LANES - e))[None]
        xmid, f_moe, top_e, rank, gate_t, counts = _post_attention(
            oa, ob, oc, xa, mod3, g_norm2[l][None], w_out[l].astype(BF16), wr2, br, q0)

        ntok = top_e.shape[1]
        n_tiles = (TOP_K * ntok) // E_TM + e
        pos_flat, tile_expert, tile_block, tile_rows, tile_end = _routing_tables(top_e, rank, counts, n_tiles)
        xs = _dispatch(pos_flat, tile_end, f_moe, n_tiles * E_TM)
        ys = _experts(tile_expert + l * e, tile_block, tile_rows, xs, w1p_all, b1p_all, w2_all, b2_all)
        xa = _combine(pos_flat, ys, gate_t, xmid, mod3, g_final[None], q0, last)
    return xa
```
